```python
import math
import jax, jax.numpy as jnp
from jax import lax
import numpy as np

D_MODEL = 1024
BATCH = 32
SEQ = 2048
DEPTH = 4

HEAD_DIM = 64
D_SSM = D_MODEL
SSM_HEADS = D_SSM // HEAD_DIM
SSM_GROUPS = 4
SSM_STATE = 128
CONV_WIDTH = 4
CHUNK = 128
D_ATT = D_MODEL
ATT_HEADS = D_ATT // HEAD_DIM
Q_BLOCK = 128
D_MIX = D_SSM + D_ATT
D_CONV = D_SSM + 2 * SSM_GROUPS * SSM_STATE
D_IN_PROJ = D_SSM + D_CONV + SSM_HEADS + 3 * D_ATT
D_FF = 4 * D_MODEL
EPS = 1e-5
DT_MIN = 1e-3
DT_MAX = 1e-1

kernel_name = "hybrid_ssd_stickbreaking_trunk"


def rmsnorm(x, g):
    xf = x.astype(jnp.float32)
    r = lax.rsqrt(jnp.mean(xf * xf, axis=-1, keepdims=True) + EPS)
    return (xf * r * g.astype(jnp.float32)).astype(x.dtype)


def causal_depthwise_conv(u, w, b):
    c = u.shape[-1]
    out = lax.conv_general_dilated(
        u, w[:, None, :].astype(u.dtype), window_strides=(1,),
        padding=[(CONV_WIDTH - 1, 0)],
        dimension_numbers=("NWC", "WIO", "NWC"), feature_group_count=c)
    return out + b


def ssd_chunked(x, dt, a_neg, bm, cm):
    b, L = x.shape[:2]
    nc = L // CHUNK
    r = SSM_HEADS // SSM_GROUPS
    xdt = (x * dt[..., None]).reshape(b, nc, CHUNK, SSM_GROUPS, r, HEAD_DIM)
    a = (dt * a_neg).reshape(b, nc, CHUNK, SSM_GROUPS, r)
    a = jnp.moveaxis(a, 2, -1)
    a_cum = jnp.cumsum(a, axis=-1)
    bc = bm.reshape(b, nc, CHUNK, SSM_GROUPS, SSM_STATE)
    cc = cm.reshape(b, nc, CHUNK, SSM_GROUPS, SSM_STATE)
    causal = jnp.tril(jnp.ones((CHUNK, CHUNK), dtype=bool))
    seg = a_cum[..., :, None] - a_cum[..., None, :]
    lmat = jnp.exp(jnp.where(causal, seg, -jnp.inf))
    cb = jnp.einsum("bclgn,bcsgn->bcgls", cc, bc)
    w = cb[:, :, :, None] * lmat
    y_diag = jnp.einsum("bcgrls,bcsgrp->bclgrp", w, xdt)
    decay_states = jnp.exp(a_cum[..., -1:] - a_cum)
    states = jnp.einsum("bclgn,bcgrl,bclgrp->bcgrpn", bc, decay_states, xdt)
    chunk_decay = jnp.exp(a_cum[..., -1])

    def step(h, inp):
        s_c, d_c = inp
        h_new = h * d_c[..., None, None] + s_c
        return h_new, h

    h0 = jnp.zeros_like(states[:, 0])
    _, prev_states = lax.scan(step, h0, (jnp.moveaxis(states, 1, 0), jnp.moveaxis(chunk_decay, 1, 0)))
    prev_states = jnp.moveaxis(prev_states, 0, 1)
    state_decay = jnp.moveaxis(jnp.exp(a_cum), -1, 2)[..., None]
    y_off = jnp.einsum("bclgn,bcgrpn->bclgrp", cc, prev_states) * state_decay
    return (y_diag + y_off).reshape(b, L, SSM_HEADS, HEAD_DIM)


def stick_breaking_attention(q, k, v):
    L = q.shape[1]
    scale = HEAD_DIM ** -0.5
    outs = []
    for i in range(L // Q_BLOCK):
        q0 = i * Q_BLOCK
        kv_len = q0 + Q_BLOCK
        qb = q[:, q0:kv_len]
        kb = k[:, :kv_len]
        vb = v[:, :kv_len]
        logits = jnp.einsum("bthd,bshd->bhts", qb, kb).astype(jnp.float32) * scale
        t_pos = q0 + jnp.arange(Q_BLOCK)
        s_pos = jnp.arange(kv_len)
        mask = s_pos[None, :] < t_pos[:, None]
        log_beta = jax.nn.log_sigmoid(logits)
        log_1m_beta = jnp.where(mask, jax.nn.log_sigmoid(-logits), 0.0)
        suffix = lax.cumsum(log_1m_beta, axis=log_1m_beta.ndim - 1, reverse=True) - log_1m_beta
        att = jnp.where(mask, jnp.exp(log_beta + suffix), 0.0)
        outs.append(jnp.einsum("bhts,bshd->bthd", att.astype(vb.dtype), vb))
    return jnp.concatenate(outs, axis=1)


def hybrid_layer(x, norm_mix_g, w_in, conv_w, conv_b, dt_bias, a_log, d_skip,
                 ssd_norm_g, att_norm_g, w_out, norm_mlp_g, w_up, w_down):
    b, L, _ = x.shape
    h = rmsnorm(x, norm_mix_g)
    proj = h @ w_in
    splits = [D_SSM, D_SSM + D_CONV, D_SSM + D_CONV + SSM_HEADS,
              D_SSM + D_CONV + SSM_HEADS + D_ATT,
              D_SSM + D_CONV + SSM_HEADS + 2 * D_ATT]
    z, xbc, dt_raw, q, k, v = jnp.split(proj, splits, axis=-1)
    xbc = jax.nn.silu(causal_depthwise_conv(xbc, conv_w, conv_b))
    xs, bm, cm = jnp.split(xbc, [D_SSM, D_SSM + SSM_GROUPS * SSM_STATE], axis=-1)
    dt = jax.nn.softplus(dt_raw.astype(jnp.float32) + dt_bias.astype(jnp.float32))
    a_neg = -jnp.exp(a_log.astype(jnp.float32))
    xs = xs.reshape(b, L, SSM_HEADS, HEAD_DIM)
    y = ssd_chunked(xs, dt, a_neg,
                    bm.reshape(b, L, SSM_GROUPS, SSM_STATE),
                    cm.reshape(b, L, SSM_GROUPS, SSM_STATE))
    y = (y + xs * d_skip[:, None]).reshape(b, L, D_SSM)
    y_ssd = rmsnorm(y * jax.nn.silu(z), ssd_norm_g)
    y_att = stick_breaking_attention(q.reshape(b, L, ATT_HEADS, HEAD_DIM),
                                     k.reshape(b, L, ATT_HEADS, HEAD_DIM),
                                     v.reshape(b, L, ATT_HEADS, HEAD_DIM))
    y_att = rmsnorm(y_att.reshape(b, L, D_ATT), att_norm_g)
    x = x + jnp.concatenate([y_ssd, y_att], axis=-1) @ w_out
    h = rmsnorm(x, norm_mlp_g)
    x = x + jnp.square(jax.nn.relu(h @ w_up)) @ w_down
    return x


def setup_inputs(seed: int = 0) -> dict:
    key = jax.random.key(seed)
    ks = jax.random.split(key, 16)
    f32 = jnp.float32
    nrm = lambda k, shape, s: jax.random.normal(k, shape, f32) * s
    x = jax.random.normal(ks[0], (BATCH, SEQ, D_MODEL), f32)
    norm_mix_g = 1.0 + nrm(ks[1], (DEPTH, D_MODEL), 0.02)
    w_in = nrm(ks[2], (DEPTH, D_MODEL, D_IN_PROJ), D_MODEL ** -0.5)
    conv_w = nrm(ks[3], (DEPTH, CONV_WIDTH, D_CONV), CONV_WIDTH ** -0.5)
    conv_b = nrm(ks[4], (DEPTH, D_CONV), 0.02)
    dt0 = jnp.exp(jax.random.uniform(ks[5], (DEPTH, SSM_HEADS), f32,
                                     math.log(DT_MIN), math.log(DT_MAX)))
    dt_bias = dt0 + jnp.log(-jnp.expm1(-dt0))
    a_log = jnp.log(jax.random.uniform(ks[6], (DEPTH, SSM_HEADS), f32, 1.0, 16.0))
    d_skip = 1.0 + nrm(ks[7], (DEPTH, SSM_HEADS), 0.02)
    ssd_norm_g = 1.0 + nrm(ks[8], (DEPTH, D_SSM), 0.02)
    att_norm_g = 1.0 + nrm(ks[9], (DEPTH, D_ATT), 0.02)
    w_out = nrm(ks[10], (DEPTH, D_MIX, D_MODEL), 0.5 * D_MIX ** -0.5)
    norm_mlp_g = 1.0 + nrm(ks[11], (DEPTH, D_MODEL), 0.02)
    w_up = nrm(ks[12], (DEPTH, D_MODEL, D_FF), D_MODEL ** -0.5)
    w_down = nrm(ks[13], (DEPTH, D_FF, D_MODEL), 0.5 * D_FF ** -0.5)
    final_norm_g = 1.0 + nrm(ks[14], (D_MODEL,), 0.02)
    return {"x": x, "norm_mix_g": norm_mix_g, "w_in": w_in, "conv_w": conv_w,
            "conv_b": conv_b, "dt_bias": dt_bias, "a_log": a_log, "d_skip": d_skip,
            "ssd_norm_g": ssd_norm_g, "att_norm_g": att_norm_g, "w_out": w_out,
            "norm_mlp_g": norm_mlp_g, "w_up": w_up, "w_down": w_down,
            "final_norm_g": final_norm_g}


def reference(x, norm_mix_g, w_in, conv_w, conv_b, dt_bias, a_log, d_skip,
              ssd_norm_g, att_norm_g, w_out, norm_mlp_g, w_up, w_down, final_norm_g):
    for l in range(DEPTH):
        x = hybrid_layer(x, norm_mix_g[l], w_in[l], conv_w[l], conv_b[l], dt_bias[l],
                         a_log[l], d_skip[l], ssd_norm_g[l], att_norm_g[l], w_out[l],
                         norm_mlp_g[l], w_up[l], w_down[l])
    return rmsnorm(x, final_norm_g)
```

```python
import functools
import math

import jax
import jax.numpy as jnp
import numpy as np
from jax import lax
from jax.experimental import pallas as pl
from jax.experimental.pallas import tpu as pltpu

HEAD_DIM = 64
SSM_GROUPS = 4
SSM_STATE = 128
CONV_WIDTH = 4
CHUNK = 128
EPS = 1e-5

LANES = 128
SUBLANES = 8
MXU_DIM = 256
ATT_BLOCK = MXU_DIM
ROW_BLOCK = 512
VMEM_LIMIT = 56 * 1024 * 1024
NEG_BIG = -1e30
LOG2E = 1.4426950408889634

F32 = jnp.float32
BF16 = jnp.bfloat16


def _dot(a, b):
    return jnp.dot(a, b, preferred_element_type=F32)


def _dot_nt(a, b):
    return lax.dot_general(a, b, (((1,), (1,)), ((), ())), preferred_element_type=F32)


def _dot_tn(a, b):
    return lax.dot_general(a, b, (((0,), (0,)), ((), ())), preferred_element_type=F32)


def _split_bf16(x, parts):
    out = []
    r = x
    for i in range(parts):
        h = r.astype(BF16)
        out.append(h)
        if i + 1 < parts:
            r = r - h.astype(F32)
    return out


def _dot_split_lhs(x, w, parts):
    acc = None
    for h in _split_bf16(x, parts):
        d = _dot(h, w)
        acc = d if acc is None else acc + d
    return acc


def _rms_scale(x):
    return lax.rsqrt(jnp.mean(x * x, axis=-1, keepdims=True) + EPS)


def _silu(x):
    return x * jax.nn.sigmoid(x)


def _resident(shape):
    nd = len(shape)
    return pl.BlockSpec(shape, lambda *_: (0,) * nd, pipeline_mode=pl.Buffered(1))


def _in_proj_kernel(x_ref, g_ref, wz_ref, wxbc_ref, wdt_ref, wq_ref, wk_ref, wv_ref,
                    z_ref, xbc_ref, dt_ref, q_ref, k_ref, v_ref):
    x = x_ref[...]
    h = (x * _rms_scale(x) * g_ref[...]).astype(BF16)
    z_ref[...] = _dot(h, wz_ref[...])
    xbc_ref[...] = _dot(h, wxbc_ref[...])
    dt_ref[...] = _dot(h, wdt_ref[...])
    q_ref[...] = _dot(h, wq_ref[...]).astype(BF16)
    k_ref[...] = _dot(h, wk_ref[...]).astype(BF16)
    v_ref[...] = _dot(h, wv_ref[...]).astype(BF16)


def _in_proj(x, g, wz, wxbc, wdt, wq, wk, wv):
    m, d = x.shape
    tm = min(ROW_BLOCK, m)
    rows = lambda n: pl.BlockSpec((tm, n), lambda i: (i, 0))
    outs = [(wz.shape[1], F32), (wxbc.shape[1], F32), (wdt.shape[1], F32),
            (wq.shape[1], BF16), (wk.shape[1], BF16), (wv.shape[1], BF16)]
    return pl.pallas_call(
        _in_proj_kernel,
        grid=(m // tm,),
        in_specs=[rows(d), _resident(g.shape)] + [_resident(w.shape) for w in (wz, wxbc, wdt, wq, wk, wv)],
        out_specs=[rows(n) for n, _ in outs],
        out_shape=[jax.ShapeDtypeStruct((m, n), dt) for n, dt in outs],
        compiler_params=pltpu.CompilerParams(dimension_semantics=("parallel",),
                                             vmem_limit_bytes=VMEM_LIMIT),
        name="in_proj",
    )(x, g, wz, wxbc, wdt, wq, wk, wv)


def _ssd_kernel(xbc_ref, z_ref, dt_ref, cw_ref, cb_ref, dtb_ref, alog_ref, dskip_ref, gs_ref,
                expand_ref, y_ref, buf_scr, state_scr, y_scr, *, heads):
    d_ssm = heads * HEAD_DIM
    gn = SSM_GROUPS * SSM_STATE
    heads_per_group = heads // SSM_GROUPS
    tail = SUBLANES

    @pl.when(pl.program_id(1) == 0)
    def _():
        buf_scr[0:tail, :] = jnp.zeros((tail, buf_scr.shape[1]), F32)
        state_scr[...] = jnp.zeros(state_scr.shape, F32)

    xbc = xbc_ref[...]
    buf_scr[tail:tail + CHUNK, :] = xbc
    cw = cw_ref[...]
    conv = cb_ref[...] + cw[CONV_WIDTH - 1:CONV_WIDTH, :] * xbc
    for k in range(CONV_WIDTH - 1):
        off = tail - (CONV_WIDTH - 1) + k
        conv = conv + cw[k:k + 1, :] * buf_scr[off:off + CHUNK, :]
    buf_scr[0:tail, :] = xbc[CHUNK - tail:CHUNK, :]
    u = _silu(conv)
    xs = u[:, :d_ssm]
    xs_bf = xs.astype(BF16)
    bm = u[:, d_ssm:d_ssm + gn].astype(BF16)
    cm = u[:, d_ssm + gn:d_ssm + 2 * gn]

    dtr = dt_ref[...] + dtb_ref[...]
    dt = jnp.maximum(dtr, 0.0) + jnp.log1p(jnp.exp(-jnp.abs(dtr)))
    a = dt * -jnp.exp(alog_ref[...])
    row = lax.broadcasted_iota(jnp.int32, (CHUNK, CHUNK), 0)
    col = lax.broadcasted_iota(jnp.int32, (CHUNK, CHUNK), 1)
    causal = col <= row
    tril = jnp.where(causal, 1.0, 0.0).astype(BF16)
    a_cum = None
    for part in _split_bf16(a, 3):
        d = _dot(tril, part)
        a_cum = d if a_cum is None else a_cum + d
    a_cum_t = a_cum.T
    dt_t = dt.T
    a_last = a_cum[CHUNK - 1:CHUNK, :]

    lane = lax.broadcasted_iota(jnp.int32, (CHUNK, LANES), 1)
    first_head = lane < HEAD_DIM

    for pair in range(heads // 2):
        ls = slice(pair * LANES, (pair + 1) * LANES)
        rhs = jnp.concatenate([xs_bf[:, ls], state_scr[:, ls].astype(BF16)], axis=0)
        ys = []
        for h in (2 * pair, 2 * pair + 1):
            g = h // heads_per_group
            gs = slice(g * SSM_STATE, (g + 1) * SSM_STATE)
            cmg = cm[:, gs]
            cb = _dot_nt(cmg.astype(BF16), bm[:, gs])
            a_col = jnp.broadcast_to(a_cum[:, h:h + 1], (CHUNK, CHUNK))
            seg = a_col - a_cum_t[h:h + 1, :]
            lmat = jnp.exp(jnp.where(causal, seg, -jnp.inf))
            w = cb * lmat * dt_t[h:h + 1, :]
            ce = cmg * jnp.exp(a_col)
            lhs = jnp.concatenate([w.astype(BF16), ce.astype(BF16)], axis=1)
            ys.append(_dot(lhs, rhs))
        y_scr[:, ls] = jnp.where(first_head, ys[0], ys[1])

    wdec = jnp.exp(a_last - a_cum) * dt
    cdec = jnp.broadcast_to(jnp.exp(a_last), (SUBLANES, LANES))
    ex = _dot_split_lhs(jnp.concatenate([wdec, cdec], axis=0), expand_ref[...], 2)
    xw = (xs * ex[:CHUNK, :]).astype(BF16)
    cdec_x = ex[CHUNK:CHUNK + 1, :]
    width = heads_per_group * HEAD_DIM
    for g in range(SSM_GROUPS):
        gs = slice(g * SSM_STATE, (g + 1) * SSM_STATE)
        hs = slice(g * width, (g + 1) * width)
        new = _dot_tn(bm[:, gs], xw[:, hs])
        state_scr[:, hs] = state_scr[:, hs] * cdec_x[:, hs] + new

    y = y_scr[...] + xs * dskip_ref[...]
    yz = y * _silu(z_ref[...])
    y_ref[...] = (yz * _rms_scale(yz) * gs_ref[...]).astype(y_ref.dtype)


def _ssd(xbc, z, dt, cw, cb, dtb, alog, dskip, gs, expand, *, batch, heads):
    m, dconv = xbc.shape
    d_ssm = heads * HEAD_DIM
    nc = m // batch // CHUNK
    rows = lambda n: pl.BlockSpec((CHUNK, n), lambda b, c: (b * nc + c, 0))
    small = lambda a: pl.BlockSpec(a.shape, lambda b, c: (0,) * a.ndim)
    return pl.pallas_call(
        functools.partial(_ssd_kernel, heads=heads),
        grid=(batch, nc),
        in_specs=[rows(dconv), rows(d_ssm), rows(LANES)] + [small(a) for a in (cw, cb, dtb, alog, dskip, gs, expand)],
        out_specs=rows(d_ssm),
        out_shape=jax.ShapeDtypeStruct((m, d_ssm), BF16),
        scratch_shapes=[pltpu.VMEM((SUBLANES + CHUNK, dconv), F32),
                        pltpu.VMEM((SSM_STATE, d_ssm), F32),
                        pltpu.VMEM((CHUNK, d_ssm), F32)],
        compiler_params=pltpu.CompilerParams(dimension_semantics=("parallel", "arbitrary"),
                                             vmem_limit_bytes=VMEM_LIMIT),
        name="ssd",
    )(xbc, z, dt, cw, cb, dtb, alog, dskip, gs, expand)


def _attn_items(nq):
    qi = [i for i in range(nq) for _ in range(i + 1)]
    kj = [j for i in range(nq) for j in range(i, -1, -1)]
    return qi, kj


def _attn_kernel(qi_ref, kj_ref, q_ref, k_ref, v_ref, madd_ref, msuf_ref, o_ref,
                 s_scr, lb_scr, cs_scr, tot_scr, carry_scr, acc_scr, *, n_items, logit_scale):
    tb = ATT_BLOCK
    for scr in (s_scr, lb_scr, cs_scr, tot_scr, carry_scr, acc_scr):
        scr[...] = jnp.zeros(scr.shape, F32)
    lane = lax.broadcasted_iota(jnp.int32, (tb, LANES), 1)
    first_head = lane < HEAD_DIM
    msuf = msuf_ref[...]

    def body(t, _):
        tc = jnp.maximum(t - 2, 0)
        qc = qi_ref[tc]
        kc = kj_ref[tc]
        vblk = v_ref[pl.ds(pl.multiple_of(kc * tb, tb), tb), :]
        restart = kc == qc
        pvs = []
        scales = []
        for h in range(2):
            att = jnp.exp2(lb_scr[h] - cs_scr[h]).astype(BF16)
            pvs.append(_dot(att, vblk))
            carry = jnp.where(restart, 0.0, carry_scr[h])
            scales.append(jnp.exp2(-carry))
            carry_scr[h] = carry + tot_scr[h]
        contrib = jnp.where(first_head, pvs[0] * scales[0], pvs[1] * scales[1])
        acc = jnp.where(restart, 0.0, acc_scr[...]) + contrib
        acc_scr[...] = acc
        o_ref[pl.ds(pl.multiple_of(qc * tb, tb), tb), :] = acc

        tbi = jnp.clip(t - 1, 0, n_items - 1)
        diag = (qi_ref[tbi] == kj_ref[tbi]).astype(jnp.int32)
        madd = madd_ref[diag]
        for h in range(2):
            x = s_scr[h] * logit_scale + madd
            sp = jnp.log2(1.0 + jnp.exp2(-jnp.abs(x)))
            lb_scr[h] = jnp.minimum(x, 0.0) - sp
            p = jnp.maximum(x, 0.0) + sp
            hi = pltpu.bitcast(pltpu.bitcast(p, jnp.uint32) & jnp.uint32(0xFFFF0000), F32)
            lo = p - hi
            cs = _dot(hi, msuf) + _dot(lo, msuf)
            cs_scr[h] = cs
            tot_scr[h] = jnp.broadcast_to(cs[:, 0:1] + p[:, 0:1], (tb, LANES))

        ta = jnp.minimum(t, n_items - 1)
        qblk = q_ref[pl.ds(pl.multiple_of(qi_ref[ta] * tb, tb), tb), :]
        kblk = k_ref[pl.ds(pl.multiple_of(kj_ref[ta] * tb, tb), tb), :]
        zero = jnp.zeros_like(qblk)
        s_scr[0] = _dot_nt(jnp.where(first_head, qblk, zero), kblk)
        s_scr[1] = _dot_nt(jnp.where(first_head, zero, qblk), kblk)
        return 0

    lax.fori_loop(0, n_items + 2, body, 0)


def _attention(q, k, v, *, batch):
    m, d_att = q.shape
    seq = m // batch
    tb = ATT_BLOCK
    nq = seq // tb
    qi, kj = _attn_items(nq)
    n_items = len(qi)
    row = np.arange(tb)[:, None]
    col = np.arange(tb)[None, :]
    madd = np.stack([np.zeros((tb, tb), np.float32),
                     np.where(col < row, 0.0, NEG_BIG).astype(np.float32)])
    msuf = (row > col).astype(np.float32)
    blk = pl.BlockSpec((seq, LANES), lambda b, hp, *_: (b, hp))
    const = lambda a: pl.BlockSpec(a.shape, lambda b, hp, *_: (0,) * a.ndim)
    kernel = functools.partial(_attn_kernel, n_items=n_items,
                               logit_scale=float(HEAD_DIM ** -0.5 * LOG2E))
    return pl.pallas_call(
        kernel,
        grid_spec=pltpu.PrefetchScalarGridSpec(
            num_scalar_prefetch=2,
            grid=(batch, d_att // LANES),
            in_specs=[blk, blk, blk, const(madd), const(msuf)],
            out_specs=blk,
            scratch_shapes=[pltpu.VMEM((2, tb, tb), F32),
                            pltpu.VMEM((2, tb, tb), F32),
                            pltpu.VMEM((2, tb, tb), F32),
                            pltpu.VMEM((2, tb, LANES), F32),
                            pltpu.VMEM((2, tb, LANES), F32),
                            pltpu.VMEM((tb, LANES), F32)]),
        out_shape=jax.ShapeDtypeStruct((m, d_att), F32),
        compiler_params=pltpu.CompilerParams(dimension_semantics=("parallel", "parallel"),
                                             vmem_limit_bytes=VMEM_LIMIT),
        name="attention",
    )(jnp.asarray(qi, jnp.int32), jnp.asarray(kj, jnp.int32), q, k, v,
      jnp.asarray(madd), jnp.asarray(msuf))


def _out_mlp_kernel(x_ref, ys_ref, ya_ref, ga_ref, wo_ref, gm_ref, wu_ref, wd_ref, gf_ref,
                    o_ref, u_scr, *, final_norm):
    d_ssm = ys_ref.shape[1]
    ya = ya_ref[...]
    ya = (ya * _rms_scale(ya) * ga_ref[...]).astype(BF16)
    x1 = x_ref[...] + _dot(ys_ref[...], wo_ref[:d_ssm, :]) + _dot(ya, wo_ref[d_ssm:, :])
    h = (x1 * _rms_scale(x1) * gm_ref[...]).astype(BF16)
    d_ff = wu_ref.shape[1]
    step = min(d_ff, 4 * MXU_DIM)
    for c in range(0, d_ff, step):
        up = jnp.maximum(_dot(h, wu_ref[:, c:c + step]), 0.0)
        u_scr[:, c:c + step] = (up * up).astype(BF16)
    out = x1 + _dot(u_scr[...], wd_ref[...])
    if final_norm:
        out = out * _rms_scale(out) * gf_ref[...]
    o_ref[...] = out


def _out_mlp(x, ys, ya, ga, wo, gm, wu, wd, gf, *, final_norm):
    m, d = x.shape
    tm = min(ROW_BLOCK, m)
    rows = lambda n: pl.BlockSpec((tm, n), lambda i: (i, 0))
    return pl.pallas_call(
        functools.partial(_out_mlp_kernel, final_norm=final_norm),
        grid=(m // tm,),
        in_specs=[rows(d), rows(ys.shape[1]), rows(ya.shape[1]), _resident(ga.shape), _resident(wo.shape),
                  _resident(gm.shape), _resident(wu.shape), _resident(wd.shape), _resident(gf.shape)],
        out_specs=rows(d),
        out_shape=jax.ShapeDtypeStruct((m, d), F32),
        scratch_shapes=[pltpu.VMEM((tm, wu.shape[1]), BF16)],
        compiler_params=pltpu.CompilerParams(dimension_semantics=("parallel",),
                                             vmem_limit_bytes=VMEM_LIMIT),
        name="out_mlp",
    )(x, ys, ya, ga, wo, gm, wu, wd, gf)


def kernel(x, norm_mix_g, w_in, conv_w, conv_b, dt_bias, a_log, d_skip, ssd_norm_g, att_norm_g,
           w_out, norm_mlp_g, w_up, w_down, final_norm_g):
    batch, seq, d_model = x.shape
    depth = w_in.shape[0]
    heads = dt_bias.shape[1]
    d_ssm = heads * HEAD_DIM
    d_conv = conv_w.shape[2]
    d_att = (w_in.shape[2] - d_ssm - d_conv - heads) // 3
    assert seq % ATT_BLOCK == 0 and seq % CHUNK == 0 and heads <= LANES and heads % (2 * SSM_GROUPS) == 0
    assert d_conv == d_ssm + 2 * SSM_GROUPS * SSM_STATE

    o_dt = d_ssm + d_conv
    o_q = o_dt + heads
    row = lambda a: a.reshape(1, -1).astype(F32)
    pad_heads = lambda a: jnp.pad(a.astype(F32), (0, LANES - heads)).reshape(1, LANES)
    expand = jnp.asarray(np.repeat(np.eye(LANES, heads, dtype=np.float32), HEAD_DIM, axis=1), BF16)

    xf = x.reshape(batch * seq, d_model)
    for l in range(depth):
        w = w_in[l]
        wdt = jnp.pad(w[:, o_dt:o_q], ((0, 0), (0, LANES - heads))).astype(BF16)
        z, xbc, dt, q, k, v = _in_proj(
            xf, row(norm_mix_g[l]), w[:, :d_ssm].astype(BF16), w[:, d_ssm:o_dt].astype(BF16), wdt,
            w[:, o_q:o_q + d_att].astype(BF16), w[:, o_q + d_att:o_q + 2 * d_att].astype(BF16),
            w[:, o_q + 2 * d_att:].astype(BF16))
        y_ssd = _ssd(xbc, z, dt, conv_w[l].astype(F32), row(conv_b[l]), pad_heads(dt_bias[l]),
                     pad_heads(a_log[l]), row(jnp.repeat(d_skip[l], HEAD_DIM)),
                     row(ssd_norm_g[l]), expand, batch=batch, heads=heads)
        y_att = _attention(q, k, v, batch=batch)
        xf = _out_mlp(xf, y_ssd, y_att, row(att_norm_g[l]), w_out[l].astype(BF16), row(norm_mlp_g[l]),
                      w_up[l].astype(BF16), w_down[l].astype(BF16), row(final_norm_g),
                      final_norm=(l == depth - 1))
    return xf.reshape(batch, seq, d_model)
```

```python
import functools
import math

import jax
import jax.numpy as jnp
import numpy as np
from jax import lax
from jax.experimental import pallas as pl
from jax.experimental.pallas import tpu as pltpu

HEAD_DIM = 64
SSM_GROUPS = 4
SSM_STATE = 128
CONV_WIDTH = 4
CHUNK = 128
EPS = 1e-5

LANES = 128
SUBLANES = 8
MXU_DIM = 256
ATT_BLOCK = MXU_DIM
ROW_BLOCK = 512
VMEM_LIMIT = 56 * 1024 * 1024
NEG_BIG = -1e30
LOG2E = 1.4426950408889634
ATT_LOGIT_SCALE = HEAD_DIM ** -0.5 * LOG2E

F32 = jnp.float32
BF16 = jnp.bfloat16


def _dot(a, b):
    return jnp.dot(a, b, preferred_element_type=F32)


def _dot_nt(a, b):
    return lax.dot_general(a, b, (((1,), (1,)), ((), ())), preferred_element_type=F32)


def _dot_tn(a, b):
    return lax.dot_general(a, b, (((0,), (0,)), ((), ())), preferred_element_type=F32)


def _split_bf16(x, parts):
    out = []
    r = x
    for i in range(parts):
        h = r.astype(BF16)
        out.append(h)
        if i + 1 < parts:
            r = r - h.astype(F32)
    return out


def _dot_split_lhs(x, w, parts):
    acc = None
    for h in _split_bf16(x, parts):
        d = _dot(h, w)
        acc = d if acc is None else acc + d
    return acc


def _rms_scale(x):
    return lax.rsqrt(jnp.mean(x * x, axis=-1, keepdims=True) + EPS)


def _silu(x):
    return x * jax.nn.sigmoid(x)


def _resident(shape):
    nd = len(shape)
    return pl.BlockSpec(shape, lambda *_: (0,) * nd, pipeline_mode=pl.Buffered(1))


def _in_proj_kernel(x_ref, g_ref, wz_ref, wxbc_ref, wdt_ref, wq_ref, wk_ref, wv_ref,
                    z_ref, xbc_ref, dt_ref, q_ref, k_ref, v_ref, *, q_scale):
    x = x_ref[...]
    h = (x * _rms_scale(x) * g_ref[...]).astype(BF16)
    z_ref[...] = _dot(h, wz_ref[...])
    xbc_ref[...] = _dot(h, wxbc_ref[...])
    dt_ref[...] = _dot(h, wdt_ref[...])
    q_ref[...] = (_dot(h, wq_ref[...]) * q_scale).astype(BF16)
    k_ref[...] = _dot(h, wk_ref[...]).astype(BF16)
    v_ref[...] = _dot(h, wv_ref[...]).astype(BF16)


def _in_proj(x, g, wz, wxbc, wdt, wq, wk, wv):
    m, d = x.shape
    tm = min(ROW_BLOCK, m)
    rows = lambda n: pl.BlockSpec((tm, n), lambda i: (i, 0))
    outs = [(wz.shape[1], F32), (wxbc.shape[1], F32), (wdt.shape[1], F32),
            (wq.shape[1], BF16), (wk.shape[1], BF16), (wv.shape[1], BF16)]
    return pl.pallas_call(
        functools.partial(_in_proj_kernel, q_scale=ATT_LOGIT_SCALE),
        grid=(m // tm,),
        in_specs=[rows(d), _resident(g.shape)] + [_resident(w.shape) for w in (wz, wxbc, wdt, wq, wk, wv)],
        out_specs=[rows(n) for n, _ in outs],
        out_shape=[jax.ShapeDtypeStruct((m, n), dt) for n, dt in outs],
        compiler_params=pltpu.CompilerParams(dimension_semantics=("parallel",),
                                             vmem_limit_bytes=VMEM_LIMIT),
        name="in_proj",
    )(x, g, wz, wxbc, wdt, wq, wk, wv)


def _ssd_kernel(xbc_ref, z_ref, dt_ref, cw_ref, cb_ref, dtb_ref, alog_ref, dskip_ref, gs_ref,
                expand_ref, y_ref, buf_scr, state_scr, y_scr, *, heads):
    d_ssm = heads * HEAD_DIM
    gn = SSM_GROUPS * SSM_STATE
    heads_per_group = heads // SSM_GROUPS
    tail = SUBLANES

    @pl.when(pl.program_id(1) == 0)
    def _():
        buf_scr[0:tail, :] = jnp.zeros((tail, buf_scr.shape[1]), F32)
        state_scr[...] = jnp.zeros(state_scr.shape, F32)

    xbc = xbc_ref[...]
    buf_scr[tail:tail + CHUNK, :] = xbc
    cw = cw_ref[...]
    conv = cb_ref[...] + cw[CONV_WIDTH - 1:CONV_WIDTH, :] * xbc
    for k in range(CONV_WIDTH - 1):
        off = tail - (CONV_WIDTH - 1) + k
        conv = conv + cw[k:k + 1, :] * buf_scr[off:off + CHUNK, :]
    buf_scr[0:tail, :] = xbc[CHUNK - tail:CHUNK, :]
    u = _silu(conv)
    xs = u[:, :d_ssm]
    xs_bf = xs.astype(BF16)
    bm = u[:, d_ssm:d_ssm + gn].astype(BF16)
    cm = u[:, d_ssm + gn:d_ssm + 2 * gn]

    dtr = dt_ref[...] + dtb_ref[...]
    dt = jnp.maximum(dtr, 0.0) + jnp.log1p(jnp.exp(-jnp.abs(dtr)))
    a = dt * -jnp.exp(alog_ref[...])
    row = lax.broadcasted_iota(jnp.int32, (CHUNK, CHUNK), 0)
    col = lax.broadcasted_iota(jnp.int32, (CHUNK, CHUNK), 1)
    causal = col <= row
    tril = jnp.where(causal, 1.0, 0.0).astype(BF16)
    a_cum = None
    for part in _split_bf16(a, 3):
        d = _dot(tril, part)
        a_cum = d if a_cum is None else a_cum + d
    a_cum_t = a_cum.T
    dt_t = dt.T
    a_last = a_cum[CHUNK - 1:CHUNK, :]

    lane = lax.broadcasted_iota(jnp.int32, (CHUNK, LANES), 1)
    first_head = lane < HEAD_DIM

    for pair in range(heads // 2):
        ls = slice(pair * LANES, (pair + 1) * LANES)
        rhs = jnp.concatenate([xs_bf[:, ls], state_scr[:, ls].astype(BF16)], axis=0)
        ys = []
        for h in (2 * pair, 2 * pair + 1):
            g = h // heads_per_group
            gs = slice(g * SSM_STATE, (g + 1) * SSM_STATE)
            cmg = cm[:, gs]
            cb = _dot_nt(cmg.astype(BF16), bm[:, gs])
            a_col = jnp.broadcast_to(a_cum[:, h:h + 1], (CHUNK, CHUNK))
            seg = a_col - a_cum_t[h:h + 1, :]
            lmat = jnp.exp(jnp.where(causal, seg, -jnp.inf))
            w = cb * lmat * dt_t[h:h + 1, :]
            ce = cmg * jnp.exp(a_col)
            lhs = jnp.concatenate([w.astype(BF16), ce.astype(BF16)], axis=1)
            ys.append(_dot(lhs, rhs))
        y_scr[:, ls] = jnp.where(first_head, ys[0], ys[1])

    wdec = jnp.exp(a_last - a_cum) * dt
    cdec = jnp.broadcast_to(jnp.exp(a_last), (SUBLANES, LANES))
    ex = _dot_split_lhs(jnp.concatenate([wdec, cdec], axis=0), expand_ref[...], 2)
    xw = (xs * ex[:CHUNK, :]).astype(BF16)
    cdec_x = ex[CHUNK:CHUNK + 1, :]
    width = heads_per_group * HEAD_DIM
    for g in range(SSM_GROUPS):
        gs = slice(g * SSM_STATE, (g + 1) * SSM_STATE)
        hs = slice(g * width, (g + 1) * width)
        new = _dot_tn(bm[:, gs], xw[:, hs])
        state_scr[:, hs] = state_scr[:, hs] * cdec_x[:, hs] + new

    y = y_scr[...] + xs * dskip_ref[...]
    yz = y * _silu(z_ref[...])
    y_ref[...] = (yz * _rms_scale(yz) * gs_ref[...]).astype(y_ref.dtype)


def _ssd(xbc, z, dt, cw, cb, dtb, alog, dskip, gs, expand, *, batch, heads):
    m, dconv = xbc.shape
    d_ssm = heads * HEAD_DIM
    nc = m // batch // CHUNK
    rows = lambda n: pl.BlockSpec((CHUNK, n), lambda b, c: (b * nc + c, 0))
    small = lambda a: pl.BlockSpec(a.shape, lambda b, c: (0,) * a.ndim)
    return pl.pallas_call(
        functools.partial(_ssd_kernel, heads=heads),
        grid=(batch, nc),
        in_specs=[rows(dconv), rows(d_ssm), rows(LANES)] + [small(a) for a in (cw, cb, dtb, alog, dskip, gs, expand)],
        out_specs=rows(d_ssm),
        out_shape=jax.ShapeDtypeStruct((m, d_ssm), BF16),
        scratch_shapes=[pltpu.VMEM((SUBLANES + CHUNK, dconv), F32),
                        pltpu.VMEM((SSM_STATE, d_ssm), F32),
                        pltpu.VMEM((CHUNK, d_ssm), F32)],
        compiler_params=pltpu.CompilerParams(dimension_semantics=("parallel", "arbitrary"),
                                             vmem_limit_bytes=VMEM_LIMIT),
        name="ssd",
    )(xbc, z, dt, cw, cb, dtb, alog, dskip, gs, expand)


def _attn_items(nq):
    qi = [i for i in range(nq) for _ in range(i + 1)]
    kj = [j for i in range(nq) for j in range(i, -1, -1)]
    return qi, kj


ATT_PAIRS = 4
ATT_SCRATCH_PER_PAIR = 6


def _attn_kernel(qi_ref, kj_ref, q_ref, k_ref, v_ref, madd_ref, msuf_ref, o_ref, *scratch,
                 n_items):
    tb = ATT_BLOCK
    for scr in scratch:
        scr[...] = jnp.zeros(scr.shape, F32)
    first_head = lax.broadcasted_iota(jnp.int32, (tb, LANES), 1) < HEAD_DIM
    msuf = msuf_ref[...]
    sign_bit = jnp.uint32(0x80000000)

    def body(t, _):
        tc = jnp.maximum(t - 2, 0)
        qc = qi_ref[tc]
        kc = kj_ref[tc]
        restart = kc == qc
        ta = jnp.minimum(t, n_items - 1)
        qa = qi_ref[ta]
        ka = kj_ref[ta]
        madd = madd_ref[(qa == ka).astype(jnp.int32)]
        rows_qc = pl.ds(pl.multiple_of(qc * tb, tb), tb)
        rows_kc = pl.ds(pl.multiple_of(kc * tb, tb), tb)
        rows_qa = pl.ds(pl.multiple_of(qa * tb, tb), tb)
        rows_ka = pl.ds(pl.multiple_of(ka * tb, tb), tb)

        for pair in range(len(scratch) // ATT_SCRATCH_PER_PAIR):
            x_scr, lb_scr, cs_scr, tot_scr, carry_scr, acc_scr = scratch[
                pair * ATT_SCRATCH_PER_PAIR:(pair + 1) * ATT_SCRATCH_PER_PAIR]
            lanes = slice(pair * LANES, (pair + 1) * LANES)

            atts = []
            for h in range(2):
                carry = jnp.where(restart, 0.0, carry_scr[h])
                z = lb_scr[h] - cs_scr[h] - jnp.concatenate([carry, carry], axis=1)
                atts.append(jnp.exp2(z).astype(BF16))
                carry_scr[h] = carry + tot_scr[h]
            vblk = v_ref[rows_kc, lanes]
            vzero = jnp.zeros_like(vblk)
            vstack = jnp.concatenate([jnp.where(first_head, vblk, vzero),
                                      jnp.where(first_head, vzero, vblk)], axis=0)
            acc = jnp.where(restart, 0.0, acc_scr[...]) + _dot(jnp.concatenate(atts, axis=1), vstack)
            acc_scr[...] = acc
            o_ref[rows_qc, lanes] = acc

            for h in range(2):
                x = x_scr[h]
                neg_abs = pltpu.bitcast(pltpu.bitcast(x, jnp.uint32) | sign_bit, F32)
                p = jnp.maximum(x, 0.0) + jnp.log2(1.0 + jnp.exp2(neg_abs))
                lb_scr[h] = x - p
                cs = _dot(p.astype(BF16), msuf)
                cs_scr[h] = cs
                tot_scr[h] = jnp.broadcast_to(cs[:, 0:1] + p[:, 0:1], (tb, LANES))

            qblk = q_ref[rows_qa, lanes]
            kblk = k_ref[rows_ka, lanes]
            qzero = jnp.zeros_like(qblk)
            x_scr[0] = _dot_nt(jnp.where(first_head, qblk, qzero), kblk) + madd
            x_scr[1] = _dot_nt(jnp.where(first_head, qzero, qblk), kblk) + madd
        return 0

    lax.fori_loop(0, n_items + 2, body, 0)


def _attention(q, k, v, *, batch):
    m, d_att = q.shape
    seq = m // batch
    tb = ATT_BLOCK
    nq = seq // tb
    qi, kj = _attn_items(nq)
    n_items = len(qi)
    row = np.arange(tb)[:, None]
    col = np.arange(tb)[None, :]
    madd = np.stack([np.zeros((tb, tb), np.float32),
                     np.where(col < row, 0.0, NEG_BIG).astype(np.float32)])
    msuf = (row > col).astype(np.float32)
    width = ATT_PAIRS * LANES
    blk = pl.BlockSpec((seq, width), lambda b, hp, *_: (b, hp))
    const = lambda a: pl.BlockSpec(a.shape, lambda b, hp, *_: (0,) * a.ndim)
    kernel = functools.partial(_attn_kernel, n_items=n_items)
    pair_scratch = [pltpu.VMEM((2, tb, tb), F32),
                    pltpu.VMEM((2, tb, tb), F32),
                    pltpu.VMEM((2, tb, tb), F32),
                    pltpu.VMEM((2, tb, LANES), F32),
                    pltpu.VMEM((2, tb, LANES), F32),
                    pltpu.VMEM((tb, LANES), F32)]
    assert len(pair_scratch) == ATT_SCRATCH_PER_PAIR
    return pl.pallas_call(
        kernel,
        grid_spec=pltpu.PrefetchScalarGridSpec(
            num_scalar_prefetch=2,
            grid=(batch, d_att // width),
            in_specs=[blk, blk, blk, const(madd), const(msuf)],
            out_specs=blk,
            scratch_shapes=pair_scratch * ATT_PAIRS),
        out_shape=jax.ShapeDtypeStruct((m, d_att), F32),
        compiler_params=pltpu.CompilerParams(dimension_semantics=("parallel", "parallel"),
                                             vmem_limit_bytes=VMEM_LIMIT),
        name="attention",
    )(jnp.asarray(qi, jnp.int32), jnp.asarray(kj, jnp.int32), q, k, v,
      jnp.asarray(madd), jnp.asarray(msuf, BF16))


def _out_mlp_kernel(x_ref, ys_ref, ya_ref, ga_ref, wo_ref, gm_ref, wu_ref, wd_ref, gf_ref,
                    o_ref, u_scr, *, final_norm):
    d_ssm = ys_ref.shape[1]
    ya = ya_ref[...]
    ya = (ya * _rms_scale(ya) * ga_ref[...]).astype(BF16)
    x1 = x_ref[...] + _dot(ys_ref[...], wo_ref[:d_ssm, :]) + _dot(ya, wo_ref[d_ssm:, :])
    h = (x1 * _rms_scale(x1) * gm_ref[...]).astype(BF16)
    d_ff = wu_ref.shape[1]
    step = min(d_ff, 4 * MXU_DIM)
    for c in range(0, d_ff, step):
        up = jnp.maximum(_dot(h, wu_ref[:, c:c + step]), 0.0)
        u_scr[:, c:c + step] = (up * up).astype(BF16)
    out = x1 + _dot(u_scr[...], wd_ref[...])
    if final_norm:
        out = out * _rms_scale(out) * gf_ref[...]
    o_ref[...] = out


def _out_mlp(x, ys, ya, ga, wo, gm, wu, wd, gf, *, final_norm):
    m, d = x.shape
    tm = min(ROW_BLOCK, m)
    rows = lambda n: pl.BlockSpec((tm, n), lambda i: (i, 0))
    return pl.pallas_call(
        functools.partial(_out_mlp_kernel, final_norm=final_norm),
        grid=(m // tm,),
        in_specs=[rows(d), rows(ys.shape[1]), rows(ya.shape[1]), _resident(ga.shape), _resident(wo.shape),
                  _resident(gm.shape), _resident(wu.shape), _resident(wd.shape), _resident(gf.shape)],
        out_specs=rows(d),
        out_shape=jax.ShapeDtypeStruct((m, d), F32),
        scratch_shapes=[pltpu.VMEM((tm, wu.shape[1]), BF16)],
        compiler_params=pltpu.CompilerParams(dimension_semantics=("parallel",),
                                             vmem_limit_bytes=VMEM_LIMIT),
        name="out_mlp",
    )(x, ys, ya, ga, wo, gm, wu, wd, gf)


def kernel(x, norm_mix_g, w_in, conv_w, conv_b, dt_bias, a_log, d_skip, ssd_norm_g, att_norm_g,
           w_out, norm_mlp_g, w_up, w_down, final_norm_g):
    batch, seq, d_model = x.shape
    depth = w_in.shape[0]
    heads = dt_bias.shape[1]
    d_ssm = heads * HEAD_DIM
    d_conv = conv_w.shape[2]
    d_att = (w_in.shape[2] - d_ssm - d_conv - heads) // 3
    assert seq % ATT_BLOCK == 0 and seq % CHUNK == 0 and heads <= LANES and heads % (2 * SSM_GROUPS) == 0
    assert d_conv == d_ssm + 2 * SSM_GROUPS * SSM_STATE

    o_dt = d_ssm + d_conv
    o_q = o_dt + heads
    row = lambda a: a.reshape(1, -1).astype(F32)
    pad_heads = lambda a: jnp.pad(a.astype(F32), (0, LANES - heads)).reshape(1, LANES)
    expand = jnp.asarray(np.repeat(np.eye(LANES, heads, dtype=np.float32), HEAD_DIM, axis=1), BF16)

    xf = x.reshape(batch * seq, d_model)
    for l in range(depth):
        w = w_in[l]
        wdt = jnp.pad(w[:, o_dt:o_q], ((0, 0), (0, LANES - heads))).astype(BF16)
        z, xbc, dt, q, k, v = _in_proj(
            xf, row(norm_mix_g[l]), w[:, :d_ssm].astype(BF16), w[:, d_ssm:o_dt].astype(BF16), wdt,
            w[:, o_q:o_q + d_att].astype(BF16), w[:, o_q + d_att:o_q + 2 * d_att].astype(BF16),
            w[:, o_q + 2 * d_att:].astype(BF16))
        y_ssd = _ssd(xbc, z, dt, conv_w[l].astype(F32), row(conv_b[l]), pad_heads(dt_bias[l]),
                     pad_heads(a_log[l]), row(jnp.repeat(d_skip[l], HEAD_DIM)),
                     row(ssd_norm_g[l]), expand, batch=batch, heads=heads)
        y_att = _attention(q, k, v, batch=batch)
        xf = _out_mlp(xf, y_ssd, y_att, row(att_norm_g[l]), w_out[l].astype(BF16), row(norm_mlp_g[l]),
                      w_up[l].astype(BF16), w_down[l].astype(BF16), row(final_norm_g),
                      final_norm=(l == depth - 1))
    return xf.reshape(batch, seq, d_model)
```

```python
import functools
import math

import jax
import jax.numpy as jnp
import numpy as np
from jax import lax
from jax.experimental import pallas as pl
from jax.experimental.pallas import tpu as pltpu

HEAD_DIM = 64
SSM_GROUPS = 4
SSM_STATE = 128
CONV_WIDTH = 4
CHUNK = 128
EPS = 1e-5

LANES = 128
SUBLANES = 8
MXU_DIM = 256
ATT_BLOCK = MXU_DIM
ROW_BLOCK = 512
VMEM_LIMIT = 56 * 1024 * 1024
NEG_BIG = -1e30
LOG2E = 1.4426950408889634
ATT_LOGIT_SCALE = HEAD_DIM ** -0.5 * LOG2E

F32 = jnp.float32
BF16 = jnp.bfloat16


def _dot(a, b):
    return jnp.dot(a, b, preferred_element_type=F32)


def _dot_nt(a, b):
    return lax.dot_general(a, b, (((1,), (1,)), ((), ())), preferred_element_type=F32)


def _dot_tn(a, b):
    return lax.dot_general(a, b, (((0,), (0,)), ((), ())), preferred_element_type=F32)


def _split_bf16(x, parts):
    out = []
    r = x
    for i in range(parts):
        h = r.astype(BF16)
        out.append(h)
        if i + 1 < parts:
            r = r - h.astype(F32)
    return out


def _dot_split_lhs(x, w, parts):
    acc = None
    for h in _split_bf16(x, parts):
        d = _dot(h, w)
        acc = d if acc is None else acc + d
    return acc


def _rms_scale(x):
    return lax.rsqrt(jnp.mean(x * x, axis=-1, keepdims=True) + EPS)


def _silu(x):
    return x * jax.nn.sigmoid(x)


def _resident(shape):
    nd = len(shape)
    return pl.BlockSpec(shape, lambda *_: (0,) * nd, pipeline_mode=pl.Buffered(1))


def _in_proj_kernel(x_ref, g_ref, wz_ref, wxbc_ref, wdt_ref, wq_ref, wk_ref, wv_ref,
                    z_ref, xbc_ref, dt_ref, q_ref, k_ref, v_ref, *, q_scale):
    x = x_ref[...]
    h = (x * _rms_scale(x) * g_ref[...]).astype(BF16)
    z_ref[...] = _dot(h, wz_ref[...])
    xbc_ref[...] = _dot(h, wxbc_ref[...])
    dt_ref[...] = _dot(h, wdt_ref[...])
    q_ref[...] = (_dot(h, wq_ref[...]) * q_scale).astype(BF16)
    k_ref[...] = _dot(h, wk_ref[...]).astype(BF16)
    v_ref[...] = _dot(h, wv_ref[...]).astype(BF16)


def _in_proj(x, g, wz, wxbc, wdt, wq, wk, wv):
    m, d = x.shape
    tm = min(ROW_BLOCK, m)
    rows = lambda n: pl.BlockSpec((tm, n), lambda i: (i, 0))
    outs = [(wz.shape[1], F32), (wxbc.shape[1], F32), (wdt.shape[1], F32),
            (wq.shape[1], BF16), (wk.shape[1], BF16), (wv.shape[1], BF16)]
    return pl.pallas_call(
        functools.partial(_in_proj_kernel, q_scale=ATT_LOGIT_SCALE),
        grid=(m // tm,),
        in_specs=[rows(d), _resident(g.shape)] + [_resident(w.shape) for w in (wz, wxbc, wdt, wq, wk, wv)],
        out_specs=[rows(n) for n, _ in outs],
        out_shape=[jax.ShapeDtypeStruct((m, n), dt) for n, dt in outs],
        compiler_params=pltpu.CompilerParams(dimension_semantics=("parallel",),
                                             vmem_limit_bytes=VMEM_LIMIT),
        name="in_proj",
    )(x, g, wz, wxbc, wdt, wq, wk, wv)


def _ssd_kernel(xbc_ref, z_ref, dt_ref, cw_ref, cb_ref, dtb_ref, alog_ref, dskip_ref, gs_ref,
                expand_ref, y_ref, buf_scr, state_scr, y_scr, *, heads):
    d_ssm = heads * HEAD_DIM
    gn = SSM_GROUPS * SSM_STATE
    heads_per_group = heads // SSM_GROUPS
    tail = SUBLANES

    @pl.when(pl.program_id(1) == 0)
    def _():
        buf_scr[0:tail, :] = jnp.zeros((tail, buf_scr.shape[1]), F32)
        state_scr[...] = jnp.zeros(state_scr.shape, F32)

    xbc = xbc_ref[...]
    buf_scr[tail:tail + CHUNK, :] = xbc
    cw = cw_ref[...]
    conv = cb_ref[...] + cw[CONV_WIDTH - 1:CONV_WIDTH, :] * xbc
    for k in range(CONV_WIDTH - 1):
        off = tail - (CONV_WIDTH - 1) + k
        conv = conv + cw[k:k + 1, :] * buf_scr[off:off + CHUNK, :]
    buf_scr[0:tail, :] = xbc[CHUNK - tail:CHUNK, :]
    u = _silu(conv)
    xs = u[:, :d_ssm]
    xs_bf = xs.astype(BF16)
    bm = u[:, d_ssm:d_ssm + gn].astype(BF16)
    cm = u[:, d_ssm + gn:d_ssm + 2 * gn]

    dtr = dt_ref[...] + dtb_ref[...]
    dt = jnp.maximum(dtr, 0.0) + jnp.log1p(jnp.exp(-jnp.abs(dtr)))
    a = dt * -jnp.exp(alog_ref[...])
    row = lax.broadcasted_iota(jnp.int32, (CHUNK, CHUNK), 0)
    col = lax.broadcasted_iota(jnp.int32, (CHUNK, CHUNK), 1)
    causal = col <= row
    tril = jnp.where(causal, 1.0, 0.0).astype(BF16)
    a_cum = None
    for part in _split_bf16(a, 3):
        d = _dot(tril, part)
        a_cum = d if a_cum is None else a_cum + d
    a_cum_t = a_cum.T
    dt_t = dt.T
    a_last = a_cum[CHUNK - 1:CHUNK, :]

    lane = lax.broadcasted_iota(jnp.int32, (CHUNK, LANES), 1)
    first_head = lane < HEAD_DIM

    for pair in range(heads // 2):
        ls = slice(pair * LANES, (pair + 1) * LANES)
        rhs = jnp.concatenate([xs_bf[:, ls], state_scr[:, ls].astype(BF16)], axis=0)
        ys = []
        for h in (2 * pair, 2 * pair + 1):
            g = h // heads_per_group
            gs = slice(g * SSM_STATE, (g + 1) * SSM_STATE)
            cmg = cm[:, gs]
            cb = _dot_nt(cmg.astype(BF16), bm[:, gs])
            a_col = jnp.broadcast_to(a_cum[:, h:h + 1], (CHUNK, CHUNK))
            seg = a_col - a_cum_t[h:h + 1, :]
            lmat = jnp.exp(jnp.where(causal, seg, -jnp.inf))
            w = cb * lmat * dt_t[h:h + 1, :]
            ce = cmg * jnp.exp(a_col)
            lhs = jnp.concatenate([w.astype(BF16), ce.astype(BF16)], axis=1)
            ys.append(_dot(lhs, rhs))
        y_scr[:, ls] = jnp.where(first_head, ys[0], ys[1])

    wdec = jnp.exp(a_last - a_cum) * dt
    cdec = jnp.broadcast_to(jnp.exp(a_last), (SUBLANES, LANES))
    ex = _dot_split_lhs(jnp.concatenate([wdec, cdec], axis=0), expand_ref[...], 2)
    xw = (xs * ex[:CHUNK, :]).astype(BF16)
    cdec_x = ex[CHUNK:CHUNK + 1, :]
    width = heads_per_group * HEAD_DIM
    for g in range(SSM_GROUPS):
        gs = slice(g * SSM_STATE, (g + 1) * SSM_STATE)
        hs = slice(g * width, (g + 1) * width)
        new = _dot_tn(bm[:, gs], xw[:, hs])
        state_scr[:, hs] = state_scr[:, hs] * cdec_x[:, hs] + new

    y = y_scr[...] + xs * dskip_ref[...]
    yz = y * _silu(z_ref[...])
    y_ref[...] = (yz * _rms_scale(yz) * gs_ref[...]).astype(y_ref.dtype)


def _ssd(xbc, z, dt, cw, cb, dtb, alog, dskip, gs, expand, *, batch, heads):
    m, dconv = xbc.shape
    d_ssm = heads * HEAD_DIM
    nc = m // batch // CHUNK
    rows = lambda n: pl.BlockSpec((CHUNK, n), lambda b, c: (b * nc + c, 0))
    small = lambda a: pl.BlockSpec(a.shape, lambda b, c: (0,) * a.ndim)
    return pl.pallas_call(
        functools.partial(_ssd_kernel, heads=heads),
        grid=(batch, nc),
        in_specs=[rows(dconv), rows(d_ssm), rows(LANES)] + [small(a) for a in (cw, cb, dtb, alog, dskip, gs, expand)],
        out_specs=rows(d_ssm),
        out_shape=jax.ShapeDtypeStruct((m, d_ssm), BF16),
        scratch_shapes=[pltpu.VMEM((SUBLANES + CHUNK, dconv), F32),
                        pltpu.VMEM((SSM_STATE, d_ssm), F32),
                        pltpu.VMEM((CHUNK, d_ssm), F32)],
        compiler_params=pltpu.CompilerParams(dimension_semantics=("parallel", "arbitrary"),
                                             vmem_limit_bytes=VMEM_LIMIT),
        name="ssd",
    )(xbc, z, dt, cw, cb, dtb, alog, dskip, gs, expand)


ATT_PAIRS = 4
ATT_SCRATCH_PER_PAIR = 6
ATT_DEAD_CARRY = 160.0


def _min_scalar(cols):
    while len(cols) > 1:
        cols = [jnp.minimum(a, b) for a, b in zip(cols[0::2], cols[1::2])]
    col = cols[0]
    while col.shape[0] > SUBLANES:
        half = col.shape[0] // 2
        col = jnp.minimum(col[:half], col[half:])
    return jnp.min(col, axis=0, keepdims=True)[0, 0]


def _attn_kernel(q_ref, k_ref, v_ref, madd_ref, msuf_ref, o_ref, *scratch):
    tb = ATT_BLOCK
    nq = q_ref.shape[0] // tb
    n_pairs = len(scratch) // ATT_SCRATCH_PER_PAIR
    for scr in scratch:
        scr[...] = jnp.zeros(scr.shape, F32)
    first_head = lax.broadcasted_iota(jnp.int32, (tb, LANES), 1) < HEAD_DIM
    msuf = msuf_ref[...]
    sign_bit = jnp.uint32(0x80000000)

    def rows(block):
        return pl.ds(pl.multiple_of(block * tb, tb), tb)

    def cond(state):
        return state[0] < 3

    def body(state):
        drain, it, qa, ka, qb, kb, qc, kc = state
        restart_b = kb == qb
        restart_c = kc == qc
        madd = madd_ref[(qa == ka).astype(jnp.int32)]
        carry_cols = []

        for pair in range(n_pairs):
            x_scr, lb_scr, cs_scr, carry_scr, carry_c_scr, acc_scr = scratch[
                pair * ATT_SCRATCH_PER_PAIR:(pair + 1) * ATT_SCRATCH_PER_PAIR]
            lanes = slice(pair * LANES, (pair + 1) * LANES)

            atts = []
            for h in range(2):
                carry = carry_c_scr[h]
                z = lb_scr[h] - cs_scr[h] - jnp.concatenate([carry, carry], axis=1)
                atts.append(jnp.exp2(z).astype(BF16))
            vblk = v_ref[rows(kc), lanes]
            vzero = jnp.zeros_like(vblk)
            vstack = jnp.concatenate([jnp.where(first_head, vblk, vzero),
                                      jnp.where(first_head, vzero, vblk)], axis=0)
            acc = jnp.where(restart_c, 0.0, acc_scr[...]) + _dot(jnp.concatenate(atts, axis=1), vstack)
            acc_scr[...] = acc
            o_ref[rows(qc), lanes] = acc

            for h in range(2):
                x = x_scr[h]
                neg_abs = pltpu.bitcast(pltpu.bitcast(x, jnp.uint32) | sign_bit, F32)
                p = jnp.maximum(x, 0.0) + jnp.log2(1.0 + jnp.exp2(neg_abs))
                lb_scr[h] = x - p
                cs = _dot(p.astype(BF16), msuf)
                cs_scr[h] = cs
                carry = jnp.where(restart_b, 0.0, carry_scr[h])
                carry_c_scr[h] = carry
                total = cs[:, 0:1] + p[:, 0:1]
                carry_scr[h] = carry + jnp.broadcast_to(total, (tb, LANES))
                carry_cols.append(carry[:, 0:1] + total)

            qblk = q_ref[rows(qa), lanes]
            kblk = k_ref[rows(ka), lanes]
            qzero = jnp.zeros_like(qblk)
            x_scr[0] = _dot_nt(jnp.where(first_head, qblk, qzero), kblk) + madd
            x_scr[1] = _dot_nt(jnp.where(first_head, qzero, qblk), kblk) + madd

        dead = (_min_scalar(carry_cols) >= ATT_DEAD_CARRY) & (qb == qa) & (it >= 1)
        more_keys = (ka > 0) & jnp.logical_not(dead)
        q_next = jnp.where(more_keys, qa, qa + 1)
        k_next = jnp.where(more_keys, ka - 1, q_next)
        finished = (q_next >= nq) | (drain > 0)
        q_next = jnp.where(finished, qa, q_next)
        k_next = jnp.where(finished, ka, k_next)
        return (jnp.where(finished, drain + 1, 0), it + 1, q_next, k_next, qa, ka, qb, kb)

    zero = jnp.int32(0)
    lax.while_loop(cond, body, (zero,) * 8)


def _attention(q, k, v, *, batch):
    m, d_att = q.shape
    seq = m // batch
    tb = ATT_BLOCK
    row = np.arange(tb)[:, None]
    col = np.arange(tb)[None, :]
    madd = np.stack([np.zeros((tb, tb), np.float32),
                     np.where(col < row, 0.0, NEG_BIG).astype(np.float32)])
    msuf = (row > col).astype(np.float32)
    width = ATT_PAIRS * LANES
    blk = pl.BlockSpec((seq, width), lambda b, hp: (b, hp))
    const = lambda a: pl.BlockSpec(a.shape, lambda b, hp: (0,) * a.ndim)
    pair_scratch = [pltpu.VMEM((2, tb, tb), F32),
                    pltpu.VMEM((2, tb, tb), F32),
                    pltpu.VMEM((2, tb, tb), F32),
                    pltpu.VMEM((2, tb, LANES), F32),
                    pltpu.VMEM((2, tb, LANES), F32),
                    pltpu.VMEM((tb, LANES), F32)]
    assert len(pair_scratch) == ATT_SCRATCH_PER_PAIR
    return pl.pallas_call(
        _attn_kernel,
        grid=(batch, d_att // width),
        in_specs=[blk, blk, blk, const(madd), const(msuf)],
        out_specs=blk,
        scratch_shapes=pair_scratch * ATT_PAIRS,
        out_shape=jax.ShapeDtypeStruct((m, d_att), F32),
        compiler_params=pltpu.CompilerParams(dimension_semantics=("parallel", "parallel"),
                                             vmem_limit_bytes=VMEM_LIMIT),
        name="attention",
    )(q, k, v, jnp.asarray(madd), jnp.asarray(msuf, BF16))


def _out_mlp_kernel(x_ref, ys_ref, ya_ref, ga_ref, wo_ref, gm_ref, wu_ref, wd_ref, gf_ref,
                    o_ref, u_scr, *, final_norm):
    d_ssm = ys_ref.shape[1]
    ya = ya_ref[...]
    ya = (ya * _rms_scale(ya) * ga_ref[...]).astype(BF16)
    x1 = x_ref[...] + _dot(ys_ref[...], wo_ref[:d_ssm, :]) + _dot(ya, wo_ref[d_ssm:, :])
    h = (x1 * _rms_scale(x1) * gm_ref[...]).astype(BF16)
    d_ff = wu_ref.shape[1]
    step = min(d_ff, 4 * MXU_DIM)
    for c in range(0, d_ff, step):
        up = jnp.maximum(_dot(h, wu_ref[:, c:c + step]), 0.0)
        u_scr[:, c:c + step] = (up * up).astype(BF16)
    out = x1 + _dot(u_scr[...], wd_ref[...])
    if final_norm:
        out = out * _rms_scale(out) * gf_ref[...]
    o_ref[...] = out


def _out_mlp(x, ys, ya, ga, wo, gm, wu, wd, gf, *, final_norm):
    m, d = x.shape
    tm = min(ROW_BLOCK, m)
    rows = lambda n: pl.BlockSpec((tm, n), lambda i: (i, 0))
    return pl.pallas_call(
        functools.partial(_out_mlp_kernel, final_norm=final_norm),
        grid=(m // tm,),
        in_specs=[rows(d), rows(ys.shape[1]), rows(ya.shape[1]), _resident(ga.shape), _resident(wo.shape),
                  _resident(gm.shape), _resident(wu.shape), _resident(wd.shape), _resident(gf.shape)],
        out_specs=rows(d),
        out_shape=jax.ShapeDtypeStruct((m, d), F32),
        scratch_shapes=[pltpu.VMEM((tm, wu.shape[1]), BF16)],
        compiler_params=pltpu.CompilerParams(dimension_semantics=("parallel",),
                                             vmem_limit_bytes=VMEM_LIMIT),
        name="out_mlp",
    )(x, ys, ya, ga, wo, gm, wu, wd, gf)


def kernel(x, norm_mix_g, w_in, conv_w, conv_b, dt_bias, a_log, d_skip, ssd_norm_g, att_norm_g,
           w_out, norm_mlp_g, w_up, w_down, final_norm_g):
    batch, seq, d_model = x.shape
    depth = w_in.shape[0]
    heads = dt_bias.shape[1]
    d_ssm = heads * HEAD_DIM
    d_conv = conv_w.shape[2]
    d_att = (w_in.shape[2] - d_ssm - d_conv - heads) // 3
    assert seq % ATT_BLOCK == 0 and seq % CHUNK == 0 and heads <= LANES and heads % (2 * SSM_GROUPS) == 0
    assert d_conv == d_ssm + 2 * SSM_GROUPS * SSM_STATE

    o_dt = d_ssm + d_conv
    o_q = o_dt + heads
    row = lambda a: a.reshape(1, -1).astype(F32)
    pad_heads = lambda a: jnp.pad(a.astype(F32), (0, LANES - heads)).reshape(1, LANES)
    expand = jnp.asarray(np.repeat(np.eye(LANES, heads, dtype=np.float32), HEAD_DIM, axis=1), BF16)

    xf = x.reshape(batch * seq, d_model)
    for l in range(depth):
        w = w_in[l]
        wdt = jnp.pad(w[:, o_dt:o_q], ((0, 0), (0, LANES - heads))).astype(BF16)
        z, xbc, dt, q, k, v = _in_proj(
            xf, row(norm_mix_g[l]), w[:, :d_ssm].astype(BF16), w[:, d_ssm:o_dt].astype(BF16), wdt,
            w[:, o_q:o_q + d_att].astype(BF16), w[:, o_q + d_att:o_q + 2 * d_att].astype(BF16),
            w[:, o_q + 2 * d_att:].astype(BF16))
        y_ssd = _ssd(xbc, z, dt, conv_w[l].astype(F32), row(conv_b[l]), pad_heads(dt_bias[l]),
                     pad_heads(a_log[l]), row(jnp.repeat(d_skip[l], HEAD_DIM)),
                     row(ssd_norm_g[l]), expand, batch=batch, heads=heads)
        y_att = _attention(q, k, v, batch=batch)
        xf = _out_mlp(xf, y_ssd, y_att, row(att_norm_g[l]), w_out[l].astype(BF16), row(norm_mlp_g[l]),
                      w_up[l].astype(BF16), w_down[l].astype(BF16), row(final_norm_g),
                      final_norm=(l == depth - 1))
    return xf.reshape(batch, seq, d_model)
```

```python
import functools

import jax
import jax.numpy as jnp
import numpy as np
from jax import lax
from jax.experimental import pallas as pl
from jax.experimental.pallas import tpu as pltpu

HEAD_DIM = 64
SSM_GROUPS = 4
SSM_STATE = 128
CONV_WIDTH = 4
CHUNK = 128
EPS = 1e-5

LANES = 128
SUBLANES = 8
MXU_DIM = 256
ATT_BLOCK = MXU_DIM
ROW_BLOCK = 512
VMEM_LIMIT = 56 * 1024 * 1024
NEG_BIG = -1e30
LOG2E = 1.4426950408889634
ATT_LOGIT_SCALE = HEAD_DIM ** -0.5 * LOG2E

F32 = jnp.float32
BF16 = jnp.bfloat16


def _dot(a, b):
    return jnp.dot(a, b, preferred_element_type=F32)


def _dot_nt(a, b):
    return lax.dot_general(a, b, (((1,), (1,)), ((), ())), preferred_element_type=F32)


def _dot_tn(a, b):
    return lax.dot_general(a, b, (((0,), (0,)), ((), ())), preferred_element_type=F32)


def _split_bf16(x, parts):
    out = []
    r = x
    for i in range(parts):
        h = r.astype(BF16)
        out.append(h)
        if i + 1 < parts:
            r = r - h.astype(F32)
    return out


def _dot_split_lhs(x, w, parts):
    acc = None
    for h in _split_bf16(x, parts):
        d = _dot(h, w)
        acc = d if acc is None else acc + d
    return acc


def _rms_scale(x):
    return lax.rsqrt(jnp.mean(x * x, axis=-1, keepdims=True) + EPS)


def _silu(x):
    return x * jax.nn.sigmoid(x)


def _resident(shape):
    nd = len(shape)
    return pl.BlockSpec(shape, lambda *_: (0,) * nd, pipeline_mode=pl.Buffered(1))


def _in_proj_kernel(x_ref, g_ref, wz_ref, wxbc_ref, wdt_ref, wq_ref, wk_ref, wv_ref,
                    z_ref, xbc_ref, dt_ref, q_ref, k_ref, v_ref, *, q_scale):
    x = x_ref[...]
    h = (x * _rms_scale(x) * g_ref[...]).astype(BF16)
    z_ref[...] = _dot(h, wz_ref[...])
    xbc_ref[...] = _dot(h, wxbc_ref[...])
    dt_ref[...] = _dot(h, wdt_ref[...])
    q_ref[...] = (_dot(h, wq_ref[...]) * q_scale).astype(BF16)
    k_ref[...] = _dot(h, wk_ref[...]).astype(BF16)
    v_ref[...] = _dot(h, wv_ref[...]).astype(BF16)


def _in_proj(x, g, wz, wxbc, wdt, wq, wk, wv):
    m, d = x.shape
    tm = min(ROW_BLOCK, m)
    rows = lambda n: pl.BlockSpec((tm, n), lambda i: (i, 0))
    outs = [(wz.shape[1], F32), (wxbc.shape[1], F32), (wdt.shape[1], F32),
            (wq.shape[1], BF16), (wk.shape[1], BF16), (wv.shape[1], BF16)]
    return pl.pallas_call(
        functools.partial(_in_proj_kernel, q_scale=ATT_LOGIT_SCALE),
        grid=(m // tm,),
        in_specs=[rows(d), _resident(g.shape)] + [_resident(w.shape) for w in (wz, wxbc, wdt, wq, wk, wv)],
        out_specs=[rows(n) for n, _ in outs],
        out_shape=[jax.ShapeDtypeStruct((m, n), dt) for n, dt in outs],
        compiler_params=pltpu.CompilerParams(dimension_semantics=("parallel",),
                                             vmem_limit_bytes=VMEM_LIMIT),
        name="in_proj",
    )(x, g, wz, wxbc, wdt, wq, wk, wv)


def _ssd_kernel(xbc_ref, z_ref, dt_ref, cw_ref, cb_ref, dtb_ref, alog_ref, dskip_ref, gs_ref,
                expand_ref, y_ref, buf_scr, state_scr, y_scr, *, heads):
    d_ssm = heads * HEAD_DIM
    gn = SSM_GROUPS * SSM_STATE
    heads_per_group = heads // SSM_GROUPS
    tail = SUBLANES

    @pl.when(pl.program_id(1) == 0)
    def _():
        buf_scr[:, 0:tail, :] = jnp.zeros((buf_scr.shape[0], tail, LANES), F32)
        state_scr[...] = jnp.zeros(state_scr.shape, F32)

    cw = cw_ref[...]
    cbias = cb_ref[...]
    convs = []
    for j in range(buf_scr.shape[0]):
        ls = slice(j * LANES, (j + 1) * LANES)
        xj = xbc_ref[:, ls]
        buf_scr[j, tail:tail + CHUNK, :] = xj
        cj = cbias[:, ls] + cw[CONV_WIDTH - 1:CONV_WIDTH, ls] * xj
        for k in range(CONV_WIDTH - 1):
            off = tail - (CONV_WIDTH - 1) + k
            cj = cj + cw[k:k + 1, ls] * buf_scr[j, off:off + CHUNK, :]
        buf_scr[j, 0:tail, :] = xj[CHUNK - tail:CHUNK, :]
        convs.append(cj)
    conv = jnp.concatenate(convs, axis=1)
    u = _silu(conv)
    xs = u[:, :d_ssm]
    xs_bf = xs.astype(BF16)
    bm = u[:, d_ssm:d_ssm + gn].astype(BF16)
    cm = u[:, d_ssm + gn:d_ssm + 2 * gn]

    dtr = dt_ref[...] + dtb_ref[...]
    dt = jnp.maximum(dtr, 0.0) + jnp.log1p(jnp.exp(-jnp.abs(dtr)))
    a = dt * (-LOG2E * jnp.exp(alog_ref[...]))
    row = lax.broadcasted_iota(jnp.int32, (CHUNK, CHUNK), 0)
    col = lax.broadcasted_iota(jnp.int32, (CHUNK, CHUNK), 1)
    causal = col <= row
    tril = jnp.where(causal, 1.0, 0.0).astype(BF16)
    a_cum = None
    for part in _split_bf16(a, 3):
        d = _dot(tril, part)
        a_cum = d if a_cum is None else a_cum + d
    a_cum_t = a_cum.T
    dt_t = dt.T
    a_last = a_cum[CHUNK - 1:CHUNK, :]
    e_cum = jnp.exp2(a_cum)

    lane = lax.broadcasted_iota(jnp.int32, (CHUNK, LANES), 1)
    first_head = lane < HEAD_DIM

    for pair in range(heads // 2):
        ls = slice(pair * LANES, (pair + 1) * LANES)
        rhs = jnp.concatenate([xs_bf[:, ls], state_scr[:, ls].astype(BF16)], axis=0)
        ys = []
        for h in (2 * pair, 2 * pair + 1):
            g = h // heads_per_group
            gs = slice(g * SSM_STATE, (g + 1) * SSM_STATE)
            cmg = cm[:, gs]
            cb = _dot_nt(cmg.astype(BF16), bm[:, gs])
            a_col = jnp.broadcast_to(a_cum[:, h:h + 1], (CHUNK, CHUNK))
            seg = a_col - a_cum_t[h:h + 1, :]
            lmat = jnp.exp2(jnp.where(causal, seg, -jnp.inf))
            w = cb * lmat * dt_t[h:h + 1, :]
            ce = cmg * jnp.broadcast_to(e_cum[:, h:h + 1], (CHUNK, CHUNK))
            lhs = jnp.concatenate([w.astype(BF16), ce.astype(BF16)], axis=1)
            ys.append(_dot(lhs, rhs))
        y_scr[:, ls] = jnp.where(first_head, ys[0], ys[1])

    wdec = jnp.exp2(a_last - a_cum) * dt
    cdec = jnp.broadcast_to(e_cum[CHUNK - 1:CHUNK, :], (SUBLANES, LANES))
    ex = _dot_split_lhs(jnp.concatenate([wdec, cdec], axis=0), expand_ref[...], 2)
    xw = (xs * ex[:CHUNK, :]).astype(BF16)
    cdec_x = ex[CHUNK:CHUNK + 1, :]
    width = heads_per_group * HEAD_DIM
    for g in range(SSM_GROUPS):
        gs = slice(g * SSM_STATE, (g + 1) * SSM_STATE)
        hs = slice(g * width, (g + 1) * width)
        new = _dot_tn(bm[:, gs], xw[:, hs])
        state_scr[:, hs] = state_scr[:, hs] * cdec_x[:, hs] + new

    y = y_scr[...] + xs * dskip_ref[...]
    yz = y * _silu(z_ref[...])
    y_ref[...] = (yz * _rms_scale(yz) * gs_ref[...]).astype(y_ref.dtype)


def _ssd(xbc, z, dt, cw, cb, dtb, alog, dskip, gs, expand, *, batch, heads):
    m, dconv = xbc.shape
    d_ssm = heads * HEAD_DIM
    nc = m // batch // CHUNK
    rows = lambda n: pl.BlockSpec((CHUNK, n), lambda b, c: (b * nc + c, 0))
    small = lambda a: pl.BlockSpec(a.shape, lambda b, c: (0,) * a.ndim)
    return pl.pallas_call(
        functools.partial(_ssd_kernel, heads=heads),
        grid=(batch, nc),
        in_specs=[rows(dconv), rows(d_ssm), rows(LANES)] + [small(a) for a in (cw, cb, dtb, alog, dskip, gs, expand)],
        out_specs=rows(d_ssm),
        out_shape=jax.ShapeDtypeStruct((m, d_ssm), BF16),
        scratch_shapes=[pltpu.VMEM((dconv // LANES, SUBLANES + CHUNK, LANES), F32),
                        pltpu.VMEM((SSM_STATE, d_ssm), F32),
                        pltpu.VMEM((CHUNK, d_ssm), F32)],
        compiler_params=pltpu.CompilerParams(dimension_semantics=("parallel", "arbitrary"),
                                             vmem_limit_bytes=VMEM_LIMIT),
        name="ssd",
    )(xbc, z, dt, cw, cb, dtb, alog, dskip, gs, expand)


ATT_PAIRS = 4
ATT_SCRATCH_PER_PAIR = 6
ATT_DEAD_CARRY = 160.0


def _min_scalar(cols):
    while len(cols) > 1:
        cols = [jnp.minimum(a, b) for a, b in zip(cols[0::2], cols[1::2])]
    col = cols[0]
    while col.shape[0] > SUBLANES:
        half = col.shape[0] // 2
        col = jnp.minimum(col[:half], col[half:])
    return jnp.min(col, axis=0, keepdims=True)[0, 0]


def _attn_kernel(q_ref, k_ref, v_ref, madd_ref, msuf_ref, o_ref, *scratch):
    tb = ATT_BLOCK
    nq = q_ref.shape[0] // tb
    n_pairs = len(scratch) // ATT_SCRATCH_PER_PAIR
    for scr in scratch:
        scr[...] = jnp.zeros(scr.shape, F32)
    first_head = lax.broadcasted_iota(jnp.int32, (tb, LANES), 1) < HEAD_DIM
    msuf = msuf_ref[...]
    sign_bit = jnp.uint32(0x80000000)

    def rows(block):
        return pl.ds(pl.multiple_of(block * tb, tb), tb)

    def cond(state):
        return state[0] < 2

    def body(state):
        drain, it, next_q, q0, k0, live0, q1, k1, live1, qc, kc, live_c = state
        sa = it & 1
        sb = 1 - sa
        a_is_0 = sa == 0
        qa, ka = jnp.where(a_is_0, q0, q1), jnp.where(a_is_0, k0, k1)
        qb, kb, live_b = jnp.where(a_is_0, q1, q0), jnp.where(a_is_0, k1, k0), jnp.where(a_is_0, live1, live0)
        restart_b = kb == qb
        restart_c = kc == qc
        madd = madd_ref[(qa == ka).astype(jnp.int32)]
        carry_cols = []

        for pair in range(n_pairs):
            x_scr, lb_scr, cs_scr, carry_scr, carry_c_scr, acc_scr = scratch[
                pair * ATT_SCRATCH_PER_PAIR:(pair + 1) * ATT_SCRATCH_PER_PAIR]
            lanes = slice(pair * LANES, (pair + 1) * LANES)

            atts = []
            for h in range(2):
                carry = carry_c_scr[h]
                z = lb_scr[h] - cs_scr[h] - jnp.concatenate([carry, carry], axis=1)
                atts.append(jnp.exp2(z).astype(BF16))
            vblk = v_ref[rows(kc), lanes]
            vzero = jnp.zeros_like(vblk)
            vstack = jnp.concatenate([jnp.where(first_head, vblk, vzero),
                                      jnp.where(first_head, vzero, vblk)], axis=0)
            acc = jnp.where(restart_c, 0.0, acc_scr[sa]) + _dot(jnp.concatenate(atts, axis=1), vstack)
            acc_scr[sa] = acc
            o_ref[rows(qc), lanes] = jnp.where(live_c == 1, acc, o_ref[rows(qc), lanes])

            for h in range(2):
                x = x_scr[h]
                neg_abs = pltpu.bitcast(pltpu.bitcast(x, jnp.uint32) | sign_bit, F32)
                p = jnp.maximum(x, 0.0) + jnp.log2(1.0 + jnp.exp2(neg_abs))
                lb_scr[h] = x - p
                cs = _dot(p.astype(BF16), msuf)
                cs_scr[h] = cs
                carry = jnp.where(restart_b, 0.0, carry_scr[sb, h])
                carry_c_scr[h] = carry
                total = cs[:, 0:1] + p[:, 0:1]
                carry_scr[sb, h] = carry + jnp.broadcast_to(total, (tb, LANES))
                carry_cols.append(carry[:, 0:1] + total)

            qblk = q_ref[rows(qa), lanes]
            kblk = k_ref[rows(ka), lanes]
            qzero = jnp.zeros_like(qblk)
            x_scr[0] = _dot_nt(jnp.where(first_head, qblk, qzero), kblk) + madd
            x_scr[1] = _dot_nt(jnp.where(first_head, qzero, qblk), kblk) + madd

        dead = _min_scalar(carry_cols) >= ATT_DEAD_CARRY
        more_keys = (live_b == 1) & (kb > 0) & jnp.logical_not(dead)
        new_block = jnp.logical_not(more_keys) & (next_q < nq)
        q_new = jnp.where(new_block, next_q, qb)
        k_new = jnp.where(more_keys, kb - 1, jnp.where(new_block, next_q, kb))
        live_new = (more_keys | new_block).astype(jnp.int32)
        next_q = next_q + new_block.astype(jnp.int32)
        q0, k0, live0 = (jnp.where(a_is_0, q0, q_new), jnp.where(a_is_0, k0, k_new),
                         jnp.where(a_is_0, live0, live_new))
        q1, k1, live1 = (jnp.where(a_is_0, q_new, q1), jnp.where(a_is_0, k_new, k1),
                         jnp.where(a_is_0, live_new, live1))
        drain = jnp.where((live0 == 0) & (live1 == 0), drain + 1, 0)
        return (drain, it + 1, next_q, q0, k0, live0, q1, k1, live1, qb, kb, live_b)

    zero, one = jnp.int32(0), jnp.int32(1)
    lax.while_loop(cond, body, (zero, zero, one, zero, zero, one) + (zero,) * 6)


def _attention(q, k, v, *, batch):
    m, d_att = q.shape
    seq = m // batch
    tb = ATT_BLOCK
    row = np.arange(tb)[:, None]
    col = np.arange(tb)[None, :]
    madd = np.stack([np.zeros((tb, tb), np.float32),
                     np.where(col < row, 0.0, NEG_BIG).astype(np.float32)])
    msuf = (row > col).astype(np.float32)
    width = ATT_PAIRS * LANES
    blk = pl.BlockSpec((seq, width), lambda b, hp: (b, hp))
    const = lambda a: pl.BlockSpec(a.shape, lambda b, hp: (0,) * a.ndim)
    pair_scratch = [pltpu.VMEM((2, tb, tb), F32),
                    pltpu.VMEM((2, tb, tb), F32),
                    pltpu.VMEM((2, tb, tb), F32),
                    pltpu.VMEM((2, 2, tb, LANES), F32),
                    pltpu.VMEM((2, tb, LANES), F32),
                    pltpu.VMEM((2, tb, LANES), F32)]
    assert len(pair_scratch) == ATT_SCRATCH_PER_PAIR
    return pl.pallas_call(
        _attn_kernel,
        grid=(batch, d_att // width),
        in_specs=[blk, blk, blk, const(madd), const(msuf)],
        out_specs=blk,
        scratch_shapes=pair_scratch * ATT_PAIRS,
        out_shape=jax.ShapeDtypeStruct((m, d_att), F32),
        compiler_params=pltpu.CompilerParams(dimension_semantics=("parallel", "parallel"),
                                             vmem_limit_bytes=VMEM_LIMIT),
        name="attention",
    )(q, k, v, jnp.asarray(madd), jnp.asarray(msuf, BF16))


def _out_mlp_kernel(x_ref, ys_ref, ya_ref, ga_ref, wo_ref, gm_ref, wu_ref, wd_ref, gf_ref,
                    o_ref, u_scr, *, final_norm):
    d_ssm = ys_ref.shape[1]
    ya = ya_ref[...]
    ya = (ya * _rms_scale(ya) * ga_ref[...]).astype(BF16)
    x1 = x_ref[...] + _dot(ys_ref[...], wo_ref[:d_ssm, :]) + _dot(ya, wo_ref[d_ssm:, :])
    h = (x1 * _rms_scale(x1) * gm_ref[...]).astype(BF16)
    d_ff = wu_ref.shape[1]
    step = min(d_ff, 4 * MXU_DIM)
    for c in range(0, d_ff, step):
        up = jnp.maximum(_dot(h, wu_ref[:, c:c + step]), 0.0)
        u_scr[:, c:c + step] = (up * up).astype(BF16)
    out = x1 + _dot(u_scr[...], wd_ref[...])
    if final_norm:
        out = out * _rms_scale(out) * gf_ref[...]
    o_ref[...] = out


def _out_mlp(x, ys, ya, ga, wo, gm, wu, wd, gf, *, final_norm):
    m, d = x.shape
    tm = min(ROW_BLOCK, m)
    rows = lambda n: pl.BlockSpec((tm, n), lambda i: (i, 0))
    return pl.pallas_call(
        functools.partial(_out_mlp_kernel, final_norm=final_norm),
        grid=(m // tm,),
        in_specs=[rows(d), rows(ys.shape[1]), rows(ya.shape[1]), _resident(ga.shape), _resident(wo.shape),
                  _resident(gm.shape), _resident(wu.shape), _resident(wd.shape), _resident(gf.shape)],
        out_specs=rows(d),
        out_shape=jax.ShapeDtypeStruct((m, d), F32),
        scratch_shapes=[pltpu.VMEM((tm, wu.shape[1]), BF16)],
        compiler_params=pltpu.CompilerParams(dimension_semantics=("parallel",),
                                             vmem_limit_bytes=VMEM_LIMIT),
        name="out_mlp",
    )(x, ys, ya, ga, wo, gm, wu, wd, gf)


def kernel(x, norm_mix_g, w_in, conv_w, conv_b, dt_bias, a_log, d_skip, ssd_norm_g, att_norm_g,
           w_out, norm_mlp_g, w_up, w_down, final_norm_g):
    batch, seq, d_model = x.shape
    depth = w_in.shape[0]
    heads = dt_bias.shape[1]
    d_ssm = heads * HEAD_DIM
    d_conv = conv_w.shape[2]
    d_att = (w_in.shape[2] - d_ssm - d_conv - heads) // 3
    assert seq % ATT_BLOCK == 0 and seq % CHUNK == 0 and heads <= LANES and heads % (2 * SSM_GROUPS) == 0
    assert d_conv == d_ssm + 2 * SSM_GROUPS * SSM_STATE

    o_dt = d_ssm + d_conv
    o_q = o_dt + heads
    row = lambda a: a.reshape(1, -1).astype(F32)
    pad_heads = lambda a: jnp.pad(a.astype(F32), (0, LANES - heads)).reshape(1, LANES)
    expand = jnp.asarray(np.repeat(np.eye(LANES, heads, dtype=np.float32), HEAD_DIM, axis=1), BF16)

    xf = x.reshape(batch * seq, d_model)
    for l in range(depth):
        w = w_in[l]
        wdt = jnp.pad(w[:, o_dt:o_q], ((0, 0), (0, LANES - heads))).astype(BF16)
        z, xbc, dt, q, k, v = _in_proj(
            xf, row(norm_mix_g[l]), w[:, :d_ssm].astype(BF16), w[:, d_ssm:o_dt].astype(BF16), wdt,
            w[:, o_q:o_q + d_att].astype(BF16), w[:, o_q + d_att:o_q + 2 * d_att].astype(BF16),
            w[:, o_q + 2 * d_att:].astype(BF16))
        y_ssd = _ssd(xbc, z, dt, conv_w[l].astype(F32), row(conv_b[l]), pad_heads(dt_bias[l]),
                     pad_heads(a_log[l]), row(jnp.repeat(d_skip[l], HEAD_DIM)),
                     row(ssd_norm_g[l]), expand, batch=batch, heads=heads)
        y_att = _attention(q, k, v, batch=batch)
        xf = _out_mlp(xf, y_ssd, y_att, row(att_norm_g[l]), w_out[l].astype(BF16), row(norm_mlp_g[l]),
                      w_up[l].astype(BF16), w_down[l].astype(BF16), row(final_norm_g),
                      final_norm=(l == depth - 1))
    return xf.reshape(batch, seq, d_model)
```

```python
import functools

import jax
import jax.numpy as jnp
import numpy as np
from jax import lax
from jax.experimental import pallas as pl
from jax.experimental.pallas import tpu as pltpu

HEAD_DIM = 64
SSM_GROUPS = 4
SSM_STATE = 128
CONV_WIDTH = 4
CHUNK = 128
EPS = 1e-5

LANES = 128
SUBLANES = 8
MXU_DIM = 256
ATT_BLOCK = MXU_DIM
SSD_SEQS = 2
ROW_BLOCK = 512
VMEM_LIMIT = 56 * 1024 * 1024
NEG_BIG = -1e30
LOG2E = 1.4426950408889634
ATT_LOGIT_SCALE = HEAD_DIM ** -0.5 * LOG2E

F32 = jnp.float32
BF16 = jnp.bfloat16


def _dot(a, b):
    return jnp.dot(a, b, preferred_element_type=F32)


def _dot_nt(a, b):
    return lax.dot_general(a, b, (((1,), (1,)), ((), ())), preferred_element_type=F32)


def _dot_tn(a, b):
    return lax.dot_general(a, b, (((0,), (0,)), ((), ())), preferred_element_type=F32)


def _split_bf16(x, parts):
    out = []
    r = x
    for i in range(parts):
        h = r.astype(BF16)
        out.append(h)
        if i + 1 < parts:
            r = r - h.astype(F32)
    return out


def _dot_split_lhs(x, w, parts):
    acc = None
    for h in _split_bf16(x, parts):
        d = _dot(h, w)
        acc = d if acc is None else acc + d
    return acc


def _rms_scale(x):
    return lax.rsqrt(jnp.mean(x * x, axis=-1, keepdims=True) + EPS)


def _silu(x):
    h = 0.5 * x
    return h + h * jnp.tanh(h)


def _resident(shape):
    nd = len(shape)
    return pl.BlockSpec(shape, lambda *_: (0,) * nd, pipeline_mode=pl.Buffered(1))


def _in_proj_kernel(x_ref, g_ref, wz_ref, wxbc_ref, wdt_ref, wq_ref, wk_ref, wv_ref,
                    z_ref, xbc_ref, dt_ref, q_ref, k_ref, v_ref, *, q_scale):
    x = x_ref[...]
    h = (x * _rms_scale(x) * g_ref[...]).astype(BF16)
    z_ref[...] = _dot(h, wz_ref[...])
    xbc_ref[...] = _dot(h, wxbc_ref[...])
    dt_ref[...] = _dot(h, wdt_ref[...])
    q_ref[...] = (_dot(h, wq_ref[...]) * q_scale).astype(BF16)
    k_ref[...] = _dot(h, wk_ref[...]).astype(BF16)
    v_ref[...] = _dot(h, wv_ref[...]).astype(BF16)


def _in_proj(x, g, wz, wxbc, wdt, wq, wk, wv):
    m, d = x.shape
    tm = min(ROW_BLOCK, m)
    rows = lambda n: pl.BlockSpec((tm, n), lambda i: (i, 0))
    outs = [(wz.shape[1], F32), (wxbc.shape[1], F32), (wdt.shape[1], F32),
            (wq.shape[1], BF16), (wk.shape[1], BF16), (wv.shape[1], BF16)]
    return pl.pallas_call(
        functools.partial(_in_proj_kernel, q_scale=ATT_LOGIT_SCALE),
        grid=(m // tm,),
        in_specs=[rows(d), _resident(g.shape)] + [_resident(w.shape) for w in (wz, wxbc, wdt, wq, wk, wv)],
        out_specs=[rows(n) for n, _ in outs],
        out_shape=[jax.ShapeDtypeStruct((m, n), dt) for n, dt in outs],
        compiler_params=pltpu.CompilerParams(dimension_semantics=("parallel",),
                                             vmem_limit_bytes=VMEM_LIMIT),
        name="in_proj",
    )(x, g, wz, wxbc, wdt, wq, wk, wv)


def _ssd_kernel(xbc_ref, z_ref, dt_ref, cw_ref, cb_ref, dtb_ref, alog_ref, dskip_ref, gs_ref,
                expand_ref, y_ref, buf_scr, state_scr, y_scr, *, heads):
    for s in range(xbc_ref.shape[0]):
        _ssd_chunk(xbc_ref.at[s], z_ref.at[s], dt_ref.at[s], cw_ref, cb_ref, dtb_ref, alog_ref, dskip_ref,
                   gs_ref, expand_ref, y_ref.at[s], buf_scr.at[s], state_scr.at[s], y_scr.at[s], heads=heads)


def _ssd_chunk(xbc_ref, z_ref, dt_ref, cw_ref, cb_ref, dtb_ref, alog_ref, dskip_ref, gs_ref,
               expand_ref, y_ref, buf_scr, state_scr, y_scr, *, heads):
    d_ssm = heads * HEAD_DIM
    gn = SSM_GROUPS * SSM_STATE
    heads_per_group = heads // SSM_GROUPS
    tail = SUBLANES

    @pl.when(pl.program_id(1) == 0)
    def _():
        buf_scr[:, 0:tail, :] = jnp.zeros((buf_scr.shape[0], tail, LANES), F32)
        state_scr[...] = jnp.zeros(state_scr.shape, F32)

    cw = cw_ref[...]
    cbias = cb_ref[...]
    convs = []
    for j in range(buf_scr.shape[0]):
        ls = slice(j * LANES, (j + 1) * LANES)
        xj = xbc_ref[:, ls]
        buf_scr[j, tail:tail + CHUNK, :] = xj
        cj = cbias[:, ls] + cw[CONV_WIDTH - 1:CONV_WIDTH, ls] * xj
        for k in range(CONV_WIDTH - 1):
            off = tail - (CONV_WIDTH - 1) + k
            cj = cj + cw[k:k + 1, ls] * buf_scr[j, off:off + CHUNK, :]
        buf_scr[j, 0:tail, :] = xj[CHUNK - tail:CHUNK, :]
        convs.append(cj)
    conv = jnp.concatenate(convs, axis=1)
    u = _silu(conv)
    xs = u[:, :d_ssm]
    xs_bf = xs.astype(BF16)
    bm = u[:, d_ssm:d_ssm + gn].astype(BF16)
    cm = u[:, d_ssm + gn:d_ssm + 2 * gn]

    dtr = dt_ref[...] + dtb_ref[...]
    dt = jnp.maximum(dtr, 0.0) + jnp.log1p(jnp.exp(-jnp.abs(dtr)))
    a = dt * (-LOG2E * jnp.exp(alog_ref[...]))
    row = lax.broadcasted_iota(jnp.int32, (CHUNK, CHUNK), 0)
    col = lax.broadcasted_iota(jnp.int32, (CHUNK, CHUNK), 1)
    causal = col <= row
    tril = jnp.where(causal, 1.0, 0.0).astype(BF16)
    a_cum = None
    for part in _split_bf16(a, 3):
        d = _dot(tril, part)
        a_cum = d if a_cum is None else a_cum + d
    a_cum_t = a_cum.T
    dt_t = dt.T
    a_last = a_cum[CHUNK - 1:CHUNK, :]
    e_cum = jnp.exp2(a_cum)

    lane = lax.broadcasted_iota(jnp.int32, (CHUNK, LANES), 1)
    first_head = lane < HEAD_DIM

    for pair in range(heads // 2):
        ls = slice(pair * LANES, (pair + 1) * LANES)
        rhs = jnp.concatenate([xs_bf[:, ls], state_scr[:, ls].astype(BF16)], axis=0)
        ys = []
        for h in (2 * pair, 2 * pair + 1):
            g = h // heads_per_group
            gs = slice(g * SSM_STATE, (g + 1) * SSM_STATE)
            cmg = cm[:, gs]
            cb = _dot_nt(cmg.astype(BF16), bm[:, gs])
            a_col = jnp.broadcast_to(a_cum[:, h:h + 1], (CHUNK, CHUNK))
            seg = a_col - a_cum_t[h:h + 1, :]
            lmat = jnp.exp2(jnp.where(causal, seg, -jnp.inf))
            w = cb * lmat * dt_t[h:h + 1, :]
            ce = cmg * jnp.broadcast_to(e_cum[:, h:h + 1], (CHUNK, CHUNK))
            lhs = jnp.concatenate([w.astype(BF16), ce.astype(BF16)], axis=1)
            ys.append(_dot(lhs, rhs))
        y_scr[:, ls] = jnp.where(first_head, ys[0], ys[1])

    wdec = jnp.exp2(a_last - a_cum) * dt
    cdec = jnp.broadcast_to(e_cum[CHUNK - 1:CHUNK, :], (SUBLANES, LANES))
    ex = _dot_split_lhs(jnp.concatenate([wdec, cdec], axis=0), expand_ref[...], 2)
    xw = (xs * ex[:CHUNK, :]).astype(BF16)
    cdec_x = ex[CHUNK:CHUNK + 1, :]
    width = heads_per_group * HEAD_DIM
    for g in range(SSM_GROUPS):
        gs = slice(g * SSM_STATE, (g + 1) * SSM_STATE)
        hs = slice(g * width, (g + 1) * width)
        new = _dot_tn(bm[:, gs], xw[:, hs])
        state_scr[:, hs] = state_scr[:, hs] * cdec_x[:, hs] + new

    y = y_scr[...] + xs * dskip_ref[...]
    yz = y * _silu(z_ref[...])
    y_ref[...] = (yz * _rms_scale(yz) * gs_ref[...]).astype(y_ref.dtype)


def _ssd(xbc, z, dt, cw, cb, dtb, alog, dskip, gs, expand, *, batch, heads):
    m, dconv = xbc.shape
    d_ssm = heads * HEAD_DIM
    seq = m // batch
    nc = seq // CHUNK
    ns = SSD_SEQS if batch % SSD_SEQS == 0 else 1
    per_seq = lambda a: a.reshape(batch, seq, a.shape[1])
    rows = lambda n: pl.BlockSpec((ns, CHUNK, n), lambda b, c: (b, c, 0))
    small = lambda a: pl.BlockSpec(a.shape, lambda b, c: (0,) * a.ndim)
    return pl.pallas_call(
        functools.partial(_ssd_kernel, heads=heads),
        grid=(batch // ns, nc),
        in_specs=[rows(dconv), rows(d_ssm), rows(LANES)] + [small(a) for a in (cw, cb, dtb, alog, dskip, gs, expand)],
        out_specs=rows(d_ssm),
        out_shape=jax.ShapeDtypeStruct((batch, seq, d_ssm), BF16),
        scratch_shapes=[pltpu.VMEM((ns, dconv // LANES, SUBLANES + CHUNK, LANES), F32),
                        pltpu.VMEM((ns, SSM_STATE, d_ssm), F32),
                        pltpu.VMEM((ns, CHUNK, d_ssm), F32)],
        compiler_params=pltpu.CompilerParams(dimension_semantics=("parallel", "arbitrary"),
                                             vmem_limit_bytes=VMEM_LIMIT),
        name="ssd",
    )(per_seq(xbc), per_seq(z), per_seq(dt), cw, cb, dtb, alog, dskip, gs, expand).reshape(m, d_ssm)


ATT_PAIRS = 4
ATT_SCRATCH_PER_PAIR = 6
ATT_DEAD_CARRY = 160.0


def _min_scalar(cols):
    while len(cols) > 1:
        cols = [jnp.minimum(a, b) for a, b in zip(cols[0::2], cols[1::2])]
    col = cols[0]
    while col.shape[0] > SUBLANES:
        half = col.shape[0] // 2
        col = jnp.minimum(col[:half], col[half:])
    return jnp.min(col, axis=0, keepdims=True)[0, 0]


def _attn_kernel(q_ref, k_ref, v_ref, madd_ref, msuf_ref, o_ref, *scratch):
    tb = ATT_BLOCK
    nq = q_ref.shape[0] // tb
    n_pairs = len(scratch) // ATT_SCRATCH_PER_PAIR
    group_lanes = n_pairs * LANES
    n_blocks = nq * (q_ref.shape[1] // group_lanes)
    for scr in scratch:
        scr[...] = jnp.zeros(scr.shape, F32)
    first_head = lax.broadcasted_iota(jnp.int32, (tb, LANES), 1) < HEAD_DIM
    msuf = msuf_ref[...]
    sign_bit = jnp.uint32(0x80000000)

    def rows(block):
        return pl.ds(pl.multiple_of(lax.rem(block, nq) * tb, tb), tb)

    def lanes_of(block, pair):
        return pl.ds(pl.multiple_of((block // nq) * group_lanes + pair * LANES, LANES), LANES)

    def cond(state):
        return state[0] < 2

    def body(state):
        drain, it, next_q, q0, k0, live0, q1, k1, live1, qc, kc, live_c = state
        sa = it & 1
        sb = 1 - sa
        a_is_0 = sa == 0
        qa, ka = jnp.where(a_is_0, q0, q1), jnp.where(a_is_0, k0, k1)
        qb, kb, live_b = jnp.where(a_is_0, q1, q0), jnp.where(a_is_0, k1, k0), jnp.where(a_is_0, live1, live0)
        restart_b = kb == lax.rem(qb, nq)
        restart_c = kc == lax.rem(qc, nq)
        madd = madd_ref[(lax.rem(qa, nq) == ka).astype(jnp.int32)]
        carry_cols = []

        for pair in range(n_pairs):
            x_scr, lb_scr, cs_scr, carry_scr, carry_c_scr, acc_scr = scratch[
                pair * ATT_SCRATCH_PER_PAIR:(pair + 1) * ATT_SCRATCH_PER_PAIR]

            atts = []
            for h in range(2):
                carry = carry_c_scr[h]
                z = lb_scr[h] - cs_scr[h] - jnp.concatenate([carry, carry], axis=1)
                atts.append(jnp.exp2(z).astype(BF16))
            vblk = v_ref[rows(kc), lanes_of(qc, pair)]
            vzero = jnp.zeros_like(vblk)
            vstack = jnp.concatenate([jnp.where(first_head, vblk, vzero),
                                      jnp.where(first_head, vzero, vblk)], axis=0)
            acc = jnp.where(restart_c, 0.0, acc_scr[sa]) + _dot(jnp.concatenate(atts, axis=1), vstack)
            acc_scr[sa] = acc
            out_at = (rows(qc), lanes_of(qc, pair))
            o_ref[out_at] = jnp.where(live_c == 1, acc, o_ref[out_at])

            for h in range(2):
                x = x_scr[h]
                neg_abs = pltpu.bitcast(pltpu.bitcast(x, jnp.uint32) | sign_bit, F32)
                p = jnp.maximum(x, 0.0) + jnp.log2(1.0 + jnp.exp2(neg_abs))
                lb_scr[h] = x - p
                cs = _dot(p.astype(BF16), msuf)
                cs_scr[h] = cs
                carry = jnp.where(restart_b, 0.0, carry_scr[sb, h])
                carry_c_scr[h] = carry
                total = cs[:, 0:1] + p[:, 0:1]
                carry_scr[sb, h] = carry + jnp.broadcast_to(total, (tb, LANES))
                carry_cols.append(carry[:, 0:1] + total)

            qblk = q_ref[rows(qa), lanes_of(qa, pair)]
            kblk = k_ref[rows(ka), lanes_of(qa, pair)]
            qzero = jnp.zeros_like(qblk)
            x_scr[0] = _dot_nt(jnp.where(first_head, qblk, qzero), kblk) + madd
            x_scr[1] = _dot_nt(jnp.where(first_head, qzero, qblk), kblk) + madd

        dead = _min_scalar(carry_cols) >= ATT_DEAD_CARRY
        more_keys = (live_b == 1) & (kb > 0) & jnp.logical_not(dead)
        new_block = jnp.logical_not(more_keys) & (next_q < n_blocks)
        q_new = jnp.where(new_block, next_q, qb)
        k_new = jnp.where(more_keys, kb - 1, jnp.where(new_block, lax.rem(next_q, nq), kb))
        live_new = (more_keys | new_block).astype(jnp.int32)
        next_q = next_q + new_block.astype(jnp.int32)
        q0, k0, live0 = (jnp.where(a_is_0, q0, q_new), jnp.where(a_is_0, k0, k_new),
                         jnp.where(a_is_0, live0, live_new))
        q1, k1, live1 = (jnp.where(a_is_0, q_new, q1), jnp.where(a_is_0, k_new, k1),
                         jnp.where(a_is_0, live_new, live1))
        drain = jnp.where((live0 == 0) & (live1 == 0), drain + 1, 0)
        return (drain, it + 1, next_q, q0, k0, live0, q1, k1, live1, qb, kb, live_b)

    zero, one = jnp.int32(0), jnp.int32(1)
    lax.while_loop(cond, body, (zero, zero, one, zero, zero, one) + (zero,) * 6)


def _attention(q, k, v, *, batch):
    m, d_att = q.shape
    seq = m // batch
    tb = ATT_BLOCK
    row = np.arange(tb)[:, None]
    col = np.arange(tb)[None, :]
    madd = np.stack([np.zeros((tb, tb), np.float32),
                     np.where(col < row, 0.0, NEG_BIG).astype(np.float32)])
    msuf = (row > col).astype(np.float32)
    assert d_att % (ATT_PAIRS * LANES) == 0
    blk = pl.BlockSpec((seq, d_att), lambda b: (b, 0))
    const = lambda a: pl.BlockSpec(a.shape, lambda b: (0,) * a.ndim)
    pair_scratch = [pltpu.VMEM((2, tb, tb), F32),
                    pltpu.VMEM((2, tb, tb), F32),
                    pltpu.VMEM((2, tb, tb), F32),
                    pltpu.VMEM((2, 2, tb, LANES), F32),
                    pltpu.VMEM((2, tb, LANES), F32),
                    pltpu.VMEM((2, tb, LANES), F32)]
    assert len(pair_scratch) == ATT_SCRATCH_PER_PAIR
    return pl.pallas_call(
        _attn_kernel,
        grid=(batch,),
        in_specs=[blk, blk, blk, const(madd), const(msuf)],
        out_specs=blk,
        scratch_shapes=pair_scratch * ATT_PAIRS,
        out_shape=jax.ShapeDtypeStruct((m, d_att), F32),
        compiler_params=pltpu.CompilerParams(dimension_semantics=("parallel",),
                                             vmem_limit_bytes=VMEM_LIMIT),
        name="attention",
    )(q, k, v, jnp.asarray(madd), jnp.asarray(msuf, BF16))


def _out_mlp_kernel(x_ref, ys_ref, ya_ref, ga_ref, wo_ref, gm_ref, wu_ref, wd_ref, gf_ref,
                    o_ref, u_scr, *, final_norm):
    d_ssm = ys_ref.shape[1]
    ya = ya_ref[...]
    ya = (ya * _rms_scale(ya) * ga_ref[...]).astype(BF16)
    x1 = x_ref[...] + _dot(ys_ref[...], wo_ref[:d_ssm, :]) + _dot(ya, wo_ref[d_ssm:, :])
    h = (x1 * _rms_scale(x1) * gm_ref[...]).astype(BF16)
    d_ff = wu_ref.shape[1]
    step = min(d_ff, 4 * MXU_DIM)
    for c in range(0, d_ff, step):
        up = jnp.maximum(_dot(h, wu_ref[:, c:c + step]), 0.0)
        u_scr[:, c:c + step] = (up * up).astype(BF16)
    out = x1 + _dot(u_scr[...], wd_ref[...])
    if final_norm:
        out = out * _rms_scale(out) * gf_ref[...]
    o_ref[...] = out


def _out_mlp(x, ys, ya, ga, wo, gm, wu, wd, gf, *, final_norm):
    m, d = x.shape
    tm = min(ROW_BLOCK, m)
    rows = lambda n: pl.BlockSpec((tm, n), lambda i: (i, 0))
    return pl.pallas_call(
        functools.partial(_out_mlp_kernel, final_norm=final_norm),
        grid=(m // tm,),
        in_specs=[rows(d), rows(ys.shape[1]), rows(ya.shape[1]), _resident(ga.shape), _resident(wo.shape),
                  _resident(gm.shape), _resident(wu.shape), _resident(wd.shape), _resident(gf.shape)],
        out_specs=rows(d),
        out_shape=jax.ShapeDtypeStruct((m, d), F32),
        scratch_shapes=[pltpu.VMEM((tm, wu.shape[1]), BF16)],
        compiler_params=pltpu.CompilerParams(dimension_semantics=("parallel",),
                                             vmem_limit_bytes=VMEM_LIMIT),
        name="out_mlp",
    )(x, ys, ya, ga, wo, gm, wu, wd, gf)


def kernel(x, norm_mix_g, w_in, conv_w, conv_b, dt_bias, a_log, d_skip, ssd_norm_g, att_norm_g,
           w_out, norm_mlp_g, w_up, w_down, final_norm_g):
    batch, seq, d_model = x.shape
    depth = w_in.shape[0]
    heads = dt_bias.shape[1]
    d_ssm = heads * HEAD_DIM
    d_conv = conv_w.shape[2]
    d_att = (w_in.shape[2] - d_ssm - d_conv - heads) // 3
    assert seq % ATT_BLOCK == 0 and seq % CHUNK == 0 and heads <= LANES and heads % (2 * SSM_GROUPS) == 0
    assert d_conv == d_ssm + 2 * SSM_GROUPS * SSM_STATE

    o_dt = d_ssm + d_conv
    o_q = o_dt + heads
    row = lambda a: a.reshape(1, -1).astype(F32)
    pad_heads = lambda a: jnp.pad(a.astype(F32), (0, LANES - heads)).reshape(1, LANES)
    expand = jnp.asarray(np.repeat(np.eye(LANES, heads, dtype=np.float32), HEAD_DIM, axis=1), BF16)

    xf = x.reshape(batch * seq, d_model)
    for l in range(depth):
        w = w_in[l]
        wdt = jnp.pad(w[:, o_dt:o_q], ((0, 0), (0, LANES - heads))).astype(BF16)
        z, xbc, dt, q, k, v = _in_proj(
            xf, row(norm_mix_g[l]), w[:, :d_ssm].astype(BF16), w[:, d_ssm:o_dt].astype(BF16), wdt,
            w[:, o_q:o_q + d_att].astype(BF16), w[:, o_q + d_att:o_q + 2 * d_att].astype(BF16),
            w[:, o_q + 2 * d_att:].astype(BF16))
        y_ssd = _ssd(xbc, z, dt, conv_w[l].astype(F32), row(conv_b[l]), pad_heads(dt_bias[l]),
                     pad_heads(a_log[l]), row(jnp.repeat(d_skip[l], HEAD_DIM)),
                     row(ssd_norm_g[l]), expand, batch=batch, heads=heads)
        y_att = _attention(q, k, v, batch=batch)
        xf = _out_mlp(xf, y_ssd, y_att, row(att_norm_g[l]), w_out[l].astype(BF16), row(norm_mlp_g[l]),
                      w_up[l].astype(BF16), w_down[l].astype(BF16), row(final_norm_g),
                      final_norm=(l == depth - 1))
    return xf.reshape(batch, seq, d_model)
```
